```python
import jax, jax.numpy as jnp
from jax import lax
import numpy as np

D_MODEL = 1024
BATCH = 4
SEQ = 8192
DEPTH = 1
DEC_BATCH = 32
DEC_SEQ = 64
PAST_LEN = 2048

CHUNK = 64
D_PLE = 256
D_A = 1024
CONV_W = 3
H_B = 8
K_B = 128
V_B = 128
D_B = H_B * V_B
SPLIT_SIZES = (D_A, D_A, D_A, D_A, H_B * K_B, H_B * K_B, D_B, D_B, D_MODEL, D_MODEL)
N_IN = 4 * D_A + 2 * H_B * K_B + 2 * D_B + 2 * D_MODEL
EPS = 1e-6
QK_SCALE = K_B ** -0.5

kernel_name = 'hybrid_shortconv_hgrn2_stream_step'


def rms_norm(x, g):
    xf = x.astype(jnp.float32)
    y = xf * lax.rsqrt(jnp.mean(xf * xf, axis=-1, keepdims=True) + EPS) * g.astype(jnp.float32)
    return y.astype(x.dtype)


def split_cols(z):
    out, off = [], 0
    for s in SPLIT_SIZES:
        out.append(z[..., off:off + s])
        off += s
    return out


def causal_conv3(u_ext, w):
    T = u_ext.shape[1] - (CONV_W - 1)
    out = w[0] * u_ext[:, 0:T]
    for j in range(1, CONV_W):
        out = out + w[j] * u_ext[:, j:j + T]
    return out


def hgrn2_block(S, q, k, v, logf):
    L = q.shape[1]
    b = jnp.cumsum(logf, axis=1)
    ref = b[:, (L - 1) // 2][:, None]
    qe = q * jnp.exp(b - ref)
    ke = k * jnp.exp(ref - b)
    causal = jnp.tril(jnp.ones((L, L), dtype=bool))
    att = jnp.where(causal, jnp.einsum('bthk,bshk->bhts', qe, ke), 0.0)
    o = jnp.einsum('bhts,bshv->bthv', att, v) + jnp.einsum('bthk,bhkv->bthv', q * jnp.exp(b), S)
    b_last = b[:, -1]
    S_new = jnp.exp(b_last)[..., None] * S + jnp.einsum('bshk,bshv->bhkv', k * jnp.exp(b_last[:, None] - b), v)
    return S_new, o


def hgrn2_scan(S0, q, k, v, logf):
    B, T = q.shape[0], q.shape[1]
    if T <= CHUNK:
        return hgrn2_block(S0, q, k, v, logf)
    n = T // CHUNK

    def to_chunks(a):
        return jnp.moveaxis(a.reshape((B, n, CHUNK) + a.shape[2:]), 1, 0)

    def step(S, xs):
        return hgrn2_block(S, xs[0], xs[1], xs[2], xs[3])

    S_fin, o = lax.scan(step, S0, (to_chunks(q), to_chunks(k), to_chunks(v), to_chunks(logf)))
    o = jnp.moveaxis(o, 0, 1).reshape(B, T, H_B, V_B)
    return S_fin, o


def encoder_layer(x, p, conv_hist, S0, lb, w_in, conv_w, g_pre, g_onorm, w_a_out, w_b_out, w_o,
                  g_post, g_ple, w_ple_gate, w_ple_proj):
    B, T = x.shape[0], x.shape[1]
    h = rms_norm(x, g_pre)
    proj = jnp.einsum('btd,dn->btn', h, w_in)
    vA, bA, cA, zA, q, f, iv, zB, gA, gB = split_cols(proj)
    u = cA * vA
    u_ext = jnp.concatenate([conv_hist.astype(u.dtype), u], axis=1)
    conv = causal_conv3(u_ext, conv_w)
    yA = jnp.einsum('btc,cd->btd', jax.nn.silu(zA) * bA * conv, w_a_out)
    new_conv = u_ext[:, -(CONV_W - 1):]
    fg = lb + (1.0 - lb) * jax.nn.sigmoid(f.astype(jnp.float32))
    logf = jnp.log(fg).reshape(B, T, H_B, K_B)
    kk = (1.0 - fg).reshape(B, T, H_B, K_B)
    qq = (jax.nn.silu(q.astype(jnp.float32)) * QK_SCALE).reshape(B, T, H_B, K_B)
    vv = iv.astype(jnp.float32).reshape(B, T, H_B, V_B)
    S_new, o = hgrn2_scan(S0.astype(jnp.float32), qq, kk, vv, logf)
    o = o * lax.rsqrt(jnp.mean(o * o, axis=-1, keepdims=True) + EPS)
    o = (o.reshape(B, T, D_B) * g_onorm.astype(jnp.float32)).astype(x.dtype)
    yB = jnp.einsum('btc,cd->btd', o * jax.nn.silu(zB), w_b_out)
    merged = jax.nn.sigmoid(gA) * yA + jax.nn.sigmoid(gB) * yB
    out = jnp.einsum('btd,de->bte', merged, w_o)
    x = x + rms_norm(out, g_post)
    gate = jax.nn.sigmoid(jnp.einsum('btd,de->bte', rms_norm(x, g_ple), w_ple_gate))
    x = x + jnp.einsum('btp,pd->btd', p.astype(x.dtype), w_ple_proj) * gate
    return x, new_conv, S_new


def setup_inputs(seed: int = 0) -> dict:
    key = jax.random.key(seed)
    ks = jax.random.split(key, 20)
    nrm = jax.random.normal
    f32 = jnp.float32
    return {
        'x_prompt': nrm(ks[0], (BATCH, SEQ, D_MODEL), f32),
        'x_sample': nrm(ks[1], (DEC_BATCH, DEC_SEQ, D_MODEL), f32),
        'state_conv': nrm(ks[2], (DEPTH, DEC_BATCH, CONV_W - 1, D_A), f32),
        'state_hgrn': 0.5 * nrm(ks[3], (DEPTH, DEC_BATCH, H_B, K_B, V_B), f32),
        'p_prompt': nrm(ks[4], (DEPTH, BATCH, SEQ, D_PLE), f32),
        'p_sample': nrm(ks[5], (DEPTH, DEC_BATCH, DEC_SEQ, D_PLE), f32),
        'w_in': nrm(ks[6], (DEPTH, D_MODEL, N_IN), f32) * D_MODEL ** -0.5,
        'conv_w': nrm(ks[7], (DEPTH, CONV_W, D_A), f32) * CONV_W ** -0.5,
        'lb_raw': 0.1 * nrm(ks[8], (DEPTH + 1, H_B * K_B), f32),
        'g_pre': 1.0 + 0.05 * nrm(ks[9], (DEPTH, D_MODEL), f32),
        'g_onorm': 1.0 + 0.05 * nrm(ks[10], (DEPTH, D_B), f32),
        'w_a_out': nrm(ks[11], (DEPTH, D_A, D_MODEL), f32) * D_A ** -0.5,
        'w_b_out': nrm(ks[12], (DEPTH, D_B, D_MODEL), f32) * D_B ** -0.5,
        'w_o': nrm(ks[13], (DEPTH, D_MODEL, D_MODEL), f32) * D_MODEL ** -0.5,
        'g_post': 1.0 + 0.05 * nrm(ks[14], (DEPTH, D_MODEL), f32),
        'g_ple': 1.0 + 0.05 * nrm(ks[15], (DEPTH, D_MODEL), f32),
        'w_ple_gate': nrm(ks[16], (DEPTH, D_MODEL, D_MODEL), f32) * D_MODEL ** -0.5,
        'w_ple_proj': nrm(ks[17], (DEPTH, D_PLE, D_MODEL), f32) * D_PLE ** -0.5,
    }


def reference(x_prompt, x_sample, state_conv, state_hgrn, p_prompt, p_sample, w_in, conv_w, lb_raw,
              g_pre, g_onorm, w_a_out, w_b_out, w_o, g_post, g_ple, w_ple_gate, w_ple_proj):
    lbs = jnp.cumsum(jax.nn.softmax(lb_raw.astype(jnp.float32), axis=0), axis=0)
    xp, xs = x_prompt, x_sample
    conv_p, hgrn_p, conv_s, hgrn_s = [], [], [], []
    for i in range(DEPTH):
        layer_w = (w_in[i], conv_w[i], g_pre[i], g_onorm[i], w_a_out[i], w_b_out[i], w_o[i],
                   g_post[i], g_ple[i], w_ple_gate[i], w_ple_proj[i])
        zero_conv = jnp.zeros((xp.shape[0], CONV_W - 1, D_A), xp.dtype)
        zero_S = jnp.zeros((xp.shape[0], H_B, K_B, V_B), jnp.float32)
        xp, cp, sp = encoder_layer(xp, p_prompt[i], zero_conv, zero_S, lbs[i], *layer_w)
        xs, cs, ss = encoder_layer(xs, p_sample[i], state_conv[i], state_hgrn[i], lbs[i], *layer_w)
        conv_p.append(cp)
        hgrn_p.append(sp)
        conv_s.append(cs)
        hgrn_s.append(ss)
    new_conv_prompt = jnp.stack(conv_p)
    new_hgrn_prompt = jnp.stack(hgrn_p)
    new_conv_sample = jnp.stack(conv_s)
    new_hgrn_sample = jnp.stack(hgrn_s)
    return (xp, xs, new_conv_prompt, new_hgrn_prompt, new_conv_sample, new_hgrn_sample)
```

```python
import functools

import jax
import jax.numpy as jnp
from jax import lax
from jax.experimental import pallas as pl
from jax.experimental.pallas import tpu as pltpu

D_MODEL = 1024
D_PLE = 256
D_A = 1024
CONV_W = 3
H_B = 8
K_B = 128
V_B = 128
D_B = H_B * V_B
CHUNK = 64
N_IN = 4 * D_A + 2 * H_B * K_B + 2 * D_B + 2 * D_MODEL
EPS = 1e-6
QK_SCALE = K_B ** -0.5

NSEQ = 4
TM = NSEQ * CHUNK
HIST_ROW = 8

OFF_VA, OFF_BA, OFF_CA, OFF_ZA = 0, D_A, 2 * D_A, 3 * D_A
OFF_Q = 4 * D_A
OFF_F = OFF_Q + H_B * K_B
OFF_IV = OFF_F + H_B * K_B
OFF_ZB = OFF_IV + D_B
OFF_GA = OFF_ZB + D_B
OFF_GB = OFF_GA + D_MODEL

V7X_VMEM_LIMIT_BYTES = 56 * 1024 * 1024

F32 = jnp.float32
BF16 = jnp.bfloat16


def _dot(a, b):
    return jnp.dot(a, b, preferred_element_type=F32)


def _dot_nt(a, b):
    return lax.dot_general(a, b, (((1,), (1,)), ((), ())), preferred_element_type=F32)


def _sigmoid(x):
    return 1.0 / (1.0 + jnp.exp(-x))


def _rms(x, g):
    return x * lax.rsqrt(jnp.mean(x * x, axis=-1, keepdims=True) + EPS) * g


def _layer_kernel(x_ref, p_ref, conv0_ref, s0_ref, w_in_ref, conv_w_ref, lb_raw_ref, g_pre_ref,
                  g_onorm_ref, w_a_ref, w_b_ref, w_o_ref, g_post_ref, g_ple_ref, w_pg_ref, w_pp_ref,
                  y_ref, convo_ref, so_ref,
                  st_ref, uext_ref, hb_ref, proj_ref, lhs_ref, merged_ref,
                  qe_ref, qb_ref, ke_ref, kl_ref, v_ref, vt_ref, dl_ref, o_ref):
    t = pl.program_id(1)
    nt = pl.num_programs(1)

    @pl.when(t == 0)
    def _load_state():
        for s in range(NSEQ):
            for h in range(H_B):
                st_ref[s, h] = s0_ref[s, h].T
        uext_ref[:, HIST_ROW - 2:HIST_ROW, :] = conv0_ref[...]

    x = x_ref[...].reshape(TM, D_MODEL)
    hb_ref[...] = _rms(x, g_pre_ref[...]).astype(BF16)

    proj_ref[...] = _dot(hb_ref[...], w_in_ref[:, OFF_VA:OFF_VA + 4 * D_A])
    cw = conv_w_ref[...]
    for s in range(NSEQ):
        r0 = s * CHUNK
        u = proj_ref[r0:r0 + CHUNK, 2 * D_A:3 * D_A] * proj_ref[r0:r0 + CHUNK, 0:D_A]
        uext_ref[s, HIST_ROW:HIST_ROW + CHUNK, :] = u
        conv = (cw[0:1, :] * uext_ref[s, HIST_ROW - 2:HIST_ROW - 2 + CHUNK, :]
                + cw[1:2, :] * uext_ref[s, HIST_ROW - 1:HIST_ROW - 1 + CHUNK, :]
                + cw[2:3, :] * u)
        za = proj_ref[r0:r0 + CHUNK, 3 * D_A:4 * D_A]
        ba = proj_ref[r0:r0 + CHUNK, D_A:2 * D_A]
        lhs_ref[r0:r0 + CHUNK, :] = (za * _sigmoid(za) * ba * conv).astype(BF16)
        uext_ref[s, HIST_ROW - 2:HIST_ROW, :] = uext_ref[s, HIST_ROW + CHUNK - 2:HIST_ROW + CHUNK, :]
    ya = _dot(lhs_ref[...], w_a_ref[...])
    ga = _dot(hb_ref[...], w_in_ref[:, OFF_GA:OFF_GA + D_MODEL])
    merged_ref[...] = _sigmoid(ga) * ya

    proj_ref[...] = _dot(hb_ref[...], w_in_ref[:, OFF_Q:OFF_Q + 4 * D_B])
    lbr = lb_raw_ref[...]
    lbm = jnp.max(lbr, axis=0, keepdims=True)
    lbe = jnp.exp(lbr - lbm)
    lb = lbe[0:1, :] / jnp.sum(lbe, axis=0, keepdims=True)
    row = lax.broadcasted_iota(jnp.int32, (TM, TM), 0)
    col = lax.broadcasted_iota(jnp.int32, (TM, TM), 1)
    ltri = jnp.where((row // CHUNK == col // CHUNK) & (col <= row), 1.0, 0.0).astype(BF16)
    fg = lb + (1.0 - lb) * _sigmoid(proj_ref[:, D_B:2 * D_B])
    logf = jnp.log(fg)
    logf_hi = logf.astype(BF16)
    logf_lo = (logf - logf_hi.astype(F32)).astype(BF16)
    b_all = _dot(ltri, logf_hi) + _dot(ltri, logf_lo)
    o_ref[...] = b_all
    for s in range(NSEQ):
        r0 = s * CHUNK
        b = o_ref[r0:r0 + CHUNK, :]
        bref = b[(CHUNK - 1) // 2:(CHUNK - 1) // 2 + 1, :]
        blast = b[CHUNK - 1:CHUNK, :]
        e1 = jnp.exp(b - bref)
        e2 = jnp.exp(bref - b)
        q = proj_ref[r0:r0 + CHUNK, 0:D_B]
        qe = q * _sigmoid(q) * QK_SCALE * e1
        qe_ref[r0:r0 + CHUNK, :] = qe.astype(BF16)
        qb_ref[r0:r0 + CHUNK, :] = (qe * jnp.exp(bref)).astype(BF16)
        kk = 1.0 - (lb + (1.0 - lb) * _sigmoid(proj_ref[r0:r0 + CHUNK, D_B:2 * D_B]))
        ke = kk * e2
        ke_ref[r0:r0 + CHUNK, :] = ke.astype(BF16)
        kl_ref[r0:r0 + CHUNK, :] = (ke * jnp.exp(blast - bref)).astype(BF16)
        iv = proj_ref[r0:r0 + CHUNK, 2 * D_B:3 * D_B]
        v_ref[r0:r0 + CHUNK, :] = iv.astype(BF16)
        vt_ref[s] = iv.T.astype(BF16)
        dl_ref[s:s + 1, :] = jnp.exp(blast)

    ti = lax.broadcasted_iota(jnp.int32, (CHUNK, CHUNK), 0)
    si = lax.broadcasted_iota(jnp.int32, (CHUNK, CHUNK), 1)
    causal = si <= ti
    gon = g_onorm_ref[...]
    for s in range(NSEQ):
        r0 = s * CHUNK
        for h in range(H_B):
            c0 = h * K_B
            att = _dot_nt(qe_ref[r0:r0 + CHUNK, c0:c0 + K_B], ke_ref[r0:r0 + CHUNK, c0:c0 + K_B])
            att = jnp.where(causal, att, 0.0).astype(BF16)
            st = st_ref[s, h]
            o = (_dot(att, v_ref[r0:r0 + CHUNK, c0:c0 + V_B])
                 + _dot_nt(qb_ref[r0:r0 + CHUNK, c0:c0 + K_B], st.astype(BF16)))
            dst = _dot(vt_ref[s, c0:c0 + V_B, :], kl_ref[r0:r0 + CHUNK, c0:c0 + K_B])
            st_ref[s, h] = st * dl_ref[s:s + 1, c0:c0 + K_B] + dst
            on = o * lax.rsqrt(jnp.mean(o * o, axis=-1, keepdims=True) + EPS) * gon[:, c0:c0 + V_B]
            zb = proj_ref[r0:r0 + CHUNK, 3 * D_B + c0:3 * D_B + c0 + V_B]
            lhs_ref[r0:r0 + CHUNK, c0:c0 + V_B] = (on * (zb * _sigmoid(zb))).astype(BF16)

    yb = _dot(lhs_ref[...], w_b_ref[...])
    gb = _dot(hb_ref[...], w_in_ref[:, OFF_GB:OFF_GB + D_MODEL])
    lhs_ref[...] = (merged_ref[...] + _sigmoid(gb) * yb).astype(BF16)

    out = _dot(lhs_ref[...], w_o_ref[...])
    x1 = x_ref[...].reshape(TM, D_MODEL) + _rms(out, g_post_ref[...])
    merged_ref[...] = x1
    lhs_ref[...] = _rms(x1, g_ple_ref[...]).astype(BF16)
    gate = _sigmoid(_dot(lhs_ref[...], w_pg_ref[...]))
    pp = _dot(p_ref[...].reshape(TM, D_PLE).astype(BF16), w_pp_ref[...])
    y_ref[...] = (merged_ref[...] + pp * gate).reshape(NSEQ, CHUNK, D_MODEL)

    @pl.when(t == nt - 1)
    def _store_state():
        for s in range(NSEQ):
            for h in range(H_B):
                so_ref[s, h] = st_ref[s, h].T
        convo_ref[...] = uext_ref[:, HIST_ROW - 2:HIST_ROW, :]


def _run_stream(x, p, conv0, s0, weights):
    (w_in, conv_w, lb_raw, g_pre, g_onorm, w_a, w_b, w_o, g_post, g_ple, w_pg, w_pp) = weights
    bsz, tlen, _ = x.shape
    assert bsz % NSEQ == 0 and tlen % CHUNK == 0
    grid = (bsz // NSEQ, tlen // CHUNK)

    def resident(arr):
        nd = arr.ndim
        return pl.BlockSpec(arr.shape, lambda g, t: (0,) * nd, pipeline_mode=pl.Buffered(1))

    in_specs = [
        pl.BlockSpec((NSEQ, CHUNK, D_MODEL), lambda g, t: (g, t, 0)),
        pl.BlockSpec((NSEQ, CHUNK, D_PLE), lambda g, t: (g, t, 0)),
        pl.BlockSpec((NSEQ, CONV_W - 1, D_A), lambda g, t: (g, 0, 0)),
        pl.BlockSpec((NSEQ, H_B, K_B, V_B), lambda g, t: (g, 0, 0, 0), pipeline_mode=pl.Buffered(1)),
    ] + [resident(w) for w in weights]
    out_specs = [
        pl.BlockSpec((NSEQ, CHUNK, D_MODEL), lambda g, t: (g, t, 0)),
        pl.BlockSpec((NSEQ, CONV_W - 1, D_A), lambda g, t: (g, 0, 0)),
        pl.BlockSpec((NSEQ, H_B, K_B, V_B), lambda g, t: (g, 0, 0, 0)),
    ]
    out_shape = [
        jax.ShapeDtypeStruct((bsz, tlen, D_MODEL), F32),
        jax.ShapeDtypeStruct((bsz, CONV_W - 1, D_A), F32),
        jax.ShapeDtypeStruct((bsz, H_B, K_B, V_B), F32),
    ]
    scratch = [
        pltpu.VMEM((NSEQ, H_B, V_B, K_B), F32),
        pltpu.VMEM((NSEQ, HIST_ROW + CHUNK, D_A), F32),
        pltpu.VMEM((TM, D_MODEL), BF16),
        pltpu.VMEM((TM, 4 * D_A), F32),
        pltpu.VMEM((TM, D_MODEL), BF16),
        pltpu.VMEM((TM, D_MODEL), F32),
        pltpu.VMEM((TM, D_B), BF16),
        pltpu.VMEM((TM, D_B), BF16),
        pltpu.VMEM((TM, D_B), BF16),
        pltpu.VMEM((TM, D_B), BF16),
        pltpu.VMEM((TM, D_B), BF16),
        pltpu.VMEM((NSEQ, D_B, CHUNK), BF16),
        pltpu.VMEM((8, D_B), F32),
        pltpu.VMEM((TM, D_B), F32),
    ]
    return pl.pallas_call(
        _layer_kernel,
        grid=grid,
        in_specs=in_specs,
        out_specs=out_specs,
        out_shape=out_shape,
        scratch_shapes=scratch,
        compiler_params=pltpu.CompilerParams(
            dimension_semantics=("arbitrary", "arbitrary"),
            vmem_limit_bytes=V7X_VMEM_LIMIT_BYTES,
        ),
    )(x, p, conv0, s0, *weights)


def kernel(x_prompt, x_sample, state_conv, state_hgrn, p_prompt, p_sample, w_in, conv_w, lb_raw, g_pre,
           g_onorm, w_a_out, w_b_out, w_o, g_post, g_ple, w_ple_gate, w_ple_proj):
    depth = w_in.shape[0]
    assert depth == 1
    i = 0
    weights = (
        w_in[i].astype(BF16), conv_w[i], lb_raw[i:i + 2], g_pre[i:i + 1], g_onorm[i:i + 1],
        w_a_out[i].astype(BF16), w_b_out[i].astype(BF16), w_o[i].astype(BF16),
        g_post[i:i + 1], g_ple[i:i + 1], w_ple_gate[i].astype(BF16), w_ple_proj[i].astype(BF16),
    )
    bp = x_prompt.shape[0]
    zero_conv = jnp.zeros((bp, CONV_W - 1, D_A), F32)
    zero_s = jnp.zeros((bp, H_B, K_B, V_B), F32)
    yp, cp, sp = _run_stream(x_prompt, p_prompt[i], zero_conv, zero_s, weights)
    ys, cs, ss = _run_stream(x_sample, p_sample[i], state_conv[i], state_hgrn[i], weights)
    return (yp, ys, cp[None], sp[None], cs[None], ss[None])
```

```python
import jax
import jax.numpy as jnp
from jax import lax
from jax.experimental import pallas as pl
from jax.experimental.pallas import tpu as pltpu

D_MODEL = 1024
D_PLE = 256
D_A = 1024
CONV_W = 3
H_B = 8
K_B = 128
V_B = 128
D_B = H_B * V_B
CHUNK = 64
EPS = 1e-6
QK_SCALE = K_B ** -0.5

NSEQ = 4
TM = NSEQ * CHUNK
HIST_ROW = 8
MID = (CHUNK - 1) // 2

SPLIT_NAMES = ("vA", "bA", "cA", "zA", "q", "f", "iv", "zB", "gA", "gB")
KERNEL_ORDER = ("vA", "cA", "bA", "zA", "q", "f", "iv", "zB", "gA", "gB")
GROUP_W = 1024
C_VC, C_BZ, C_QF, C_IZ, C_G = 0, 2 * GROUP_W, 4 * GROUP_W, 6 * GROUP_W, 8 * GROUP_W

V7X_VMEM_LIMIT_BYTES = 60 * 1024 * 1024

F32 = jnp.float32
BF16 = jnp.bfloat16


def _dot(a, b):
    return jnp.dot(a, b, preferred_element_type=F32)


def _dot_nt(a, b):
    return lax.dot_general(a, b, (((1,), (1,)), ((), ())), preferred_element_type=F32)


def _w(ref, c0=None, c1=None):
    words = ref[...] if c0 is None else ref[:, c0:c1]
    return pltpu.bitcast(words, BF16)


def _sigmoid(x):
    return 1.0 / (1.0 + jnp.exp(-x))


def _rms(x, g):
    return x * lax.rsqrt(jnp.mean(x * x, axis=-1, keepdims=True) + EPS) * g


def _make_kernel(n_tiles, nt, zero_state):
    def body(x_ref, p_ref, conv0_ref, s0_ref, w_in_ref, conv_w_ref, lb_raw_ref, g_pre_ref,
             g_onorm_ref, w_a_ref, w_b_ref, w_o_ref, g_post_ref, g_ple_ref, w_pg_ref, w_pp_ref,
             y_ref, convo_ref, so_ref,
             uext_ref, hb_ref, pj0_ref, pj1_ref, lhs1_ref, lhs2_ref, ball_ref,
             xs_ref, ma_ref, sgb_ref, sz_ref, qe_ref, ke_ref, v_ref, vt_ref, rows_ref):
        i = pl.program_id(0)
        j1 = jnp.minimum(i, n_tiles - 1)
        j2 = jnp.maximum(i - 1, 0)
        t1 = j1 % nt
        t2 = j2 % nt

        @pl.when(i == 0)
        def _zero_staging():
            for r in (xs_ref, ma_ref, sgb_ref, sz_ref, qe_ref, ke_ref, v_ref, vt_ref, rows_ref):
                r[...] = jnp.zeros(r.shape, r.dtype)

        @pl.when(t2 == 0)
        def _load_state():
            for s in range(NSEQ):
                for h in range(H_B):
                    if zero_state:
                        so_ref[s, h] = jnp.zeros((V_B, K_B), F32)
                    else:
                        so_ref[s, h] = s0_ref[s, h].T

        @pl.when(t1 == 0)
        def _load_conv():
            uext_ref[:, HIST_ROW - 2:HIST_ROW, :] = conv0_ref[...]

        ti = lax.broadcasted_iota(jnp.int32, (CHUNK, CHUNK), 0)
        si = lax.broadcasted_iota(jnp.int32, (CHUNK, CHUNK), 1)
        causal = si <= ti
        gon = g_onorm_ref[...]
        for s in range(NSEQ):
            r0 = s * CHUNK
            for h in range(H_B):
                c0 = h * K_B
                qe = qe_ref[r0:r0 + CHUNK, c0:c0 + K_B]
                ke = ke_ref[r0:r0 + CHUNK, c0:c0 + K_B]
                att = jnp.where(causal, _dot_nt(qe, ke), 0.0).astype(BF16)
                st = so_ref[s, h]
                e_ref = rows_ref[s, 0:1, c0:c0 + K_B]
                e_last = rows_ref[s, 1:2, c0:c0 + K_B]
                d_last = rows_ref[s, 2:3, c0:c0 + K_B]
                o = (_dot(att, v_ref[r0:r0 + CHUNK, c0:c0 + V_B])
                     + _dot_nt(qe, (st * e_ref).astype(BF16)))
                dst = _dot(vt_ref[s, c0:c0 + V_B, :], ke) * e_last
                so_ref[s, h] = st * d_last + dst
                on = o * lax.rsqrt(jnp.mean(o * o, axis=-1, keepdims=True) + EPS) * gon[:, c0:c0 + V_B]
                lhs2_ref[r0:r0 + CHUNK, c0:c0 + V_B] = (on * sz_ref[r0:r0 + CHUNK, c0:c0 + V_B]).astype(BF16)

        yb = _dot(lhs2_ref[...], _w(w_b_ref))
        lhs2_ref[...] = (ma_ref[...] + sgb_ref[...] * yb).astype(BF16)
        out = _dot(lhs2_ref[...], _w(w_o_ref))
        x1 = xs_ref[...] + _rms(out, g_post_ref[...])
        y_ref[...] = x1.reshape(NSEQ, CHUNK, D_MODEL)
        lhs2_ref[...] = _rms(x1, g_ple_ref[...]).astype(BF16)
        gate = _sigmoid(_dot(lhs2_ref[...], _w(w_pg_ref)))
        pp = _dot(p_ref[...].reshape(TM, D_PLE).astype(BF16), _w(w_pp_ref))
        y_ref[...] = y_ref[...] + (pp * gate).reshape(NSEQ, CHUNK, D_MODEL)

        x = x_ref[...].reshape(TM, D_MODEL)
        xs_ref[...] = x
        hb_ref[...] = _rms(x, g_pre_ref[...]).astype(BF16)

        pj0_ref[...] = _dot(hb_ref[...], _w(w_in_ref, C_VC, C_VC + 2 * GROUP_W))
        pj1_ref[...] = _dot(hb_ref[...], _w(w_in_ref, C_BZ, C_BZ + 2 * GROUP_W))
        cw = conv_w_ref[...]
        for s in range(NSEQ):
            r0 = s * CHUNK
            u = pj0_ref[r0:r0 + CHUNK, 0:GROUP_W] * pj0_ref[r0:r0 + CHUNK, GROUP_W:2 * GROUP_W]
            uext_ref[s, HIST_ROW:HIST_ROW + CHUNK, :] = u
            conv = (cw[0:1, :] * uext_ref[s, HIST_ROW - 2:HIST_ROW - 2 + CHUNK, :]
                    + cw[1:2, :] * uext_ref[s, HIST_ROW - 1:HIST_ROW - 1 + CHUNK, :]
                    + cw[2:3, :] * u)
            ba = pj1_ref[r0:r0 + CHUNK, 0:GROUP_W]
            za = pj1_ref[r0:r0 + CHUNK, GROUP_W:2 * GROUP_W]
            lhs1_ref[r0:r0 + CHUNK, :] = (za * _sigmoid(za) * ba * conv).astype(BF16)
            uext_ref[s, HIST_ROW - 2:HIST_ROW, :] = uext_ref[s, HIST_ROW + CHUNK - 2:HIST_ROW + CHUNK, :]

        pj0_ref[...] = _dot(hb_ref[...], _w(w_in_ref, C_QF, C_QF + 2 * GROUP_W))
        pj1_ref[...] = _dot(hb_ref[...], _w(w_in_ref, C_IZ, C_IZ + 2 * GROUP_W))
        ya = _dot(lhs1_ref[...], _w(w_a_ref))

        lbr = lb_raw_ref[...]
        lbe = jnp.exp(lbr - jnp.max(lbr, axis=0, keepdims=True))
        lb = lbe[0:1, :] / jnp.sum(lbe, axis=0, keepdims=True)
        fg = lb + (1.0 - lb) * _sigmoid(pj0_ref[:, GROUP_W:2 * GROUP_W])
        logf = jnp.log(fg)
        pj0_ref[:, GROUP_W:2 * GROUP_W] = 1.0 - fg
        logf_hi = logf.astype(BF16)
        logf_lo = (logf - logf_hi.astype(F32)).astype(BF16)
        row = lax.broadcasted_iota(jnp.int32, (TM, TM), 0)
        col = lax.broadcasted_iota(jnp.int32, (TM, TM), 1)
        ltri = jnp.where((row // CHUNK == col // CHUNK) & (col <= row), 1.0, 0.0).astype(BF16)
        ball_ref[...] = _dot(jnp.concatenate([ltri, ltri], axis=1),
                             jnp.concatenate([logf_hi, logf_lo], axis=0))
        for s in range(NSEQ):
            r0 = s * CHUNK
            b = ball_ref[r0:r0 + CHUNK, :]
            bmid = b[MID:MID + 1, :]
            blast = b[CHUNK - 1:CHUNK, :]
            q = pj0_ref[r0:r0 + CHUNK, 0:GROUP_W]
            qe_ref[r0:r0 + CHUNK, :] = (q * _sigmoid(q) * QK_SCALE * jnp.exp(b - bmid)).astype(BF16)
            kk = pj0_ref[r0:r0 + CHUNK, GROUP_W:2 * GROUP_W]
            ke_ref[r0:r0 + CHUNK, :] = (kk * jnp.exp(bmid - b)).astype(BF16)
            rows_ref[s, 0:1, :] = jnp.exp(bmid)
            rows_ref[s, 1:2, :] = jnp.exp(blast - bmid)
            rows_ref[s, 2:3, :] = jnp.exp(blast)
            iv = pj1_ref[r0:r0 + CHUNK, 0:GROUP_W]
            v_ref[r0:r0 + CHUNK, :] = iv.astype(BF16)
            vt_ref[s] = iv.T.astype(BF16)
        zb = pj1_ref[:, GROUP_W:2 * GROUP_W]
        sz_ref[...] = zb * _sigmoid(zb)

        pj0_ref[...] = _dot(hb_ref[...], _w(w_in_ref, C_G, C_G + 2 * GROUP_W))
        ma_ref[...] = _sigmoid(pj0_ref[:, 0:GROUP_W]) * ya
        sgb_ref[...] = _sigmoid(pj0_ref[:, GROUP_W:2 * GROUP_W])

        @pl.when(t2 == nt - 1)
        def _store_state():
            for s in range(NSEQ):
                for h in range(H_B):
                    so_ref[s, h] = so_ref[s, h].T

        @pl.when((t1 == nt - 1) & (i < n_tiles))
        def _store_conv():
            convo_ref[...] = uext_ref[:, HIST_ROW - 2:HIST_ROW, :]

    return body


def _run_stream(x, p, conv0, s0, weights, zero_state):
    bsz, tlen, _ = x.shape
    assert bsz % NSEQ == 0 and tlen % CHUNK == 0
    nt = tlen // CHUNK
    n_tiles = (bsz // NSEQ) * nt

    def tile1(i):
        j = jnp.minimum(i, n_tiles - 1)
        return j // nt, j % nt

    def tile2(i):
        j = jnp.maximum(i - 1, 0)
        return j // nt, j % nt

    def resident(arr):
        nd = arr.ndim
        return pl.BlockSpec(arr.shape, lambda i: (0,) * nd, pipeline_mode=pl.Buffered(1))

    in_specs = [
        pl.BlockSpec((NSEQ, CHUNK, D_MODEL), lambda i: (*tile1(i), 0)),
        pl.BlockSpec((NSEQ, CHUNK, D_PLE), lambda i: (*tile2(i), 0)),
        pl.BlockSpec((NSEQ, CONV_W - 1, D_A), lambda i: (tile1(i)[0], 0, 0)),
        (resident(s0) if zero_state else
         pl.BlockSpec((NSEQ, H_B, K_B, V_B), lambda i: (tile2(i)[0], 0, 0, 0), pipeline_mode=pl.Buffered(1))),
    ] + [resident(w) for w in weights]
    out_specs = [
        pl.BlockSpec((NSEQ, CHUNK, D_MODEL), lambda i: (*tile2(i), 0)),
        pl.BlockSpec((NSEQ, CONV_W - 1, D_A), lambda i: (tile1(i)[0], 0, 0)),
        pl.BlockSpec((NSEQ, H_B, K_B, V_B), lambda i: (tile2(i)[0], 0, 0, 0)),
    ]
    out_shape = [
        jax.ShapeDtypeStruct((bsz, tlen, D_MODEL), F32),
        jax.ShapeDtypeStruct((bsz, CONV_W - 1, D_A), F32),
        jax.ShapeDtypeStruct((bsz, H_B, K_B, V_B), F32),
    ]
    scratch = [
        pltpu.VMEM((NSEQ, HIST_ROW + CHUNK, D_A), F32),
        pltpu.VMEM((TM, D_MODEL), BF16),
        pltpu.VMEM((TM, 2 * GROUP_W), F32),
        pltpu.VMEM((TM, 2 * GROUP_W), F32),
        pltpu.VMEM((TM, D_A), BF16),
        pltpu.VMEM((TM, D_MODEL), BF16),
        pltpu.VMEM((TM, D_B), F32),
        pltpu.VMEM((TM, D_MODEL), F32),
        pltpu.VMEM((TM, D_MODEL), F32),
        pltpu.VMEM((TM, D_MODEL), F32),
        pltpu.VMEM((TM, D_B), F32),
        pltpu.VMEM((TM, D_B), BF16),
        pltpu.VMEM((TM, D_B), BF16),
        pltpu.VMEM((TM, D_B), BF16),
        pltpu.VMEM((NSEQ, D_B, CHUNK), BF16),
        pltpu.VMEM((NSEQ, 8, D_B), F32),
    ]
    return pl.pallas_call(
        _make_kernel(n_tiles, nt, zero_state),
        grid=(n_tiles + 1,),
        in_specs=in_specs,
        out_specs=out_specs,
        out_shape=out_shape,
        scratch_shapes=scratch,
        compiler_params=pltpu.CompilerParams(
            dimension_semantics=("arbitrary",),
            vmem_limit_bytes=V7X_VMEM_LIMIT_BYTES,
        ),
    )(x, p, conv0, s0, *weights)


def _pack_rows(w):
    k, n = w.shape
    wb = w.astype(BF16).reshape(k // 2, 2, n)
    return lax.bitcast_convert_type(jnp.swapaxes(wb, 1, 2), jnp.uint32)


def _regroup_w_in(w):
    cols = {name: w[:, j * GROUP_W:(j + 1) * GROUP_W] for j, name in enumerate(SPLIT_NAMES)}
    return jnp.concatenate([cols[name] for name in KERNEL_ORDER], axis=1)


def kernel(x_prompt, x_sample, state_conv, state_hgrn, p_prompt, p_sample, w_in, conv_w, lb_raw, g_pre,
           g_onorm, w_a_out, w_b_out, w_o, g_post, g_ple, w_ple_gate, w_ple_proj):
    assert w_in.shape[0] == 1
    i = 0
    weights = (
        _pack_rows(_regroup_w_in(w_in[i])), conv_w[i], lb_raw[i:i + 2], g_pre[i:i + 1], g_onorm[i:i + 1],
        _pack_rows(w_a_out[i]), _pack_rows(w_b_out[i]), _pack_rows(w_o[i]),
        g_post[i:i + 1], g_ple[i:i + 1], _pack_rows(w_ple_gate[i]), _pack_rows(w_ple_proj[i]),
    )
    bp = x_prompt.shape[0]
    zero_conv = jnp.zeros((bp, CONV_W - 1, D_A), F32)
    dummy_s = jnp.zeros((1, 1, 8, V_B), F32)
    yp, cp, sp = _run_stream(x_prompt, p_prompt[i], zero_conv, dummy_s, weights, zero_state=True)
    ys, cs, ss = _run_stream(x_sample, p_sample[i], state_conv[i], state_hgrn[i], weights, zero_state=False)
    return (yp, ys, cp[None], sp[None], cs[None], ss[None])
```

```python
import jax
import jax.numpy as jnp
from jax import lax
from jax.experimental import pallas as pl
from jax.experimental.pallas import tpu as pltpu

D_MODEL = 1024
D_PLE = 256
D_A = 1024
CONV_W = 3
H_B = 8
K_B = 128
V_B = 128
D_B = H_B * V_B
CHUNK = 64
EPS = 1e-6
QK_SCALE = K_B ** -0.5

NSEQ = 4
TM = NSEQ * CHUNK
HIST_ROW = 8
MID = (CHUNK - 1) // 2
PW = 512

SPLIT_NAMES = ("vA", "bA", "cA", "zA", "q", "f", "iv", "zB", "gA", "gB")
KERNEL_ORDER = ("vA", "cA", "bA", "zA", "q", "f", "iv", "zB", "gA", "gB")
GW = 1024
N_IN = GW * len(SPLIT_NAMES)
C_VC, C_BZ, C_QF, C_IZ, C_G = 0, 2 * GW, 4 * GW, 6 * GW, 8 * GW

V7X_VMEM_LIMIT_BYTES = 60 * 1024 * 1024

F32 = jnp.float32
BF16 = jnp.bfloat16


def _dot(a, b):
    return jnp.dot(a, b, preferred_element_type=F32)


def _dot_nt(a, b):
    return lax.dot_general(a, b, (((1,), (1,)), ((), ())), preferred_element_type=F32)


def _w(ref, c0, c1):
    return pltpu.bitcast(ref[:, c0:c1], BF16)


def _sigmoid(x):
    return 1.0 / (1.0 + jnp.exp(-x))


def _rms(x, g):
    return x * lax.rsqrt(jnp.mean(x * x, axis=-1, keepdims=True) + EPS) * g


def _weave(*piece_lists):
    pos = [0] * len(piece_lists)
    for _ in range(sum(len(pl_) for pl_ in piece_lists)):
        k = min((j for j in range(len(piece_lists)) if pos[j] < len(piece_lists[j])),
                key=lambda j: (pos[j] + 0.5) / len(piece_lists[j]))
        piece_lists[k][pos[k]]()
        pos[k] += 1


def _make_kernel(n_tiles, nt, zero_state):
    def body(x_ref, p_ref, conv0_ref, s0_ref, w_in_ref, conv_w_ref, lb_raw_ref, g_pre_ref,
             g_onorm_ref, w_a_ref, w_b_ref, w_o_ref, g_post_ref, g_ple_ref, w_pg_ref, w_pp_ref,
             y_ref, convo_ref, so_ref,
             uext_ref, hb_ref, pj0_ref, pj1_ref, lhs1_ref, lhs2_ref, mg_ref, ball_ref, hl_ref,
             xs_ref, ma_ref, sgb_ref, sz_ref, qe_ref, ke_ref, v_ref, vt_ref, rows_ref):
        i = pl.program_id(0)
        j1 = jnp.minimum(i, n_tiles - 1)
        j2 = jnp.maximum(i - 1, 0)
        t1 = j1 % nt
        t2 = j2 % nt

        @pl.when(i == 0)
        def _zero_staging():
            for r in (xs_ref, ma_ref, sgb_ref, sz_ref, qe_ref, ke_ref, v_ref, vt_ref, rows_ref):
                r[...] = jnp.zeros(r.shape, r.dtype)

        @pl.when(t2 == 0)
        def _load_state():
            for s in range(NSEQ):
                for h in range(H_B):
                    if zero_state:
                        so_ref[s, h] = jnp.zeros((V_B, K_B), F32)
                    else:
                        so_ref[s, h] = s0_ref[s, h].T

        @pl.when(t1 == 0)
        def _load_conv():
            uext_ref[:, HIST_ROW - 2:HIST_ROW, :] = conv0_ref[...]

        ti = lax.broadcasted_iota(jnp.int32, (CHUNK, CHUNK), 0)
        si = lax.broadcasted_iota(jnp.int32, (CHUNK, CHUNK), 1)
        causal = si <= ti
        lbr = lb_raw_ref[...]
        lbe = jnp.exp(lbr - jnp.max(lbr, axis=0, keepdims=True))
        lb = lbe[0:1, :] / jnp.sum(lbe, axis=0, keepdims=True)
        seqs = range(NSEQ)
        cols = range(0, D_MODEL, PW)

        def rows(s):
            return slice(s * CHUNK, (s + 1) * CHUNK)

        def hgrn(s, h):
            def piece():
                r, c = rows(s), slice(h * K_B, (h + 1) * K_B)
                qe = qe_ref[r, c]
                ke = ke_ref[r, c]
                att = jnp.where(causal, _dot_nt(qe, ke), 0.0).astype(BF16)
                st = so_ref[s, h]
                e_mid = rows_ref[s, 0:1, c]
                e_last = rows_ref[s, 1:2, c]
                d_last = rows_ref[s, 2:3, c]
                o = _dot(att, v_ref[r, c]) + _dot_nt(qe, (st * e_mid).astype(BF16))
                so_ref[s, h] = st * d_last + _dot(vt_ref[s, c, :], ke) * e_last
                on = o * lax.rsqrt(jnp.mean(o * o, axis=-1, keepdims=True) + EPS) * g_onorm_ref[:, c]
                lhs2_ref[r, c] = (on * sz_ref[r, c]).astype(BF16)
            return piece

        def branch_b_out(c0):
            def piece():
                c = slice(c0, c0 + PW)
                yb = _dot(lhs2_ref[...], _w(w_b_ref, c0, c0 + PW))
                mg_ref[:, c] = (ma_ref[:, c] + sgb_ref[:, c] * yb).astype(BF16)
            return piece

        def out_proj(c0):
            def piece():
                out = _dot(mg_ref[...], _w(w_o_ref, c0, c0 + PW))
                y_ref[:, :, c0:c0 + PW] = out.reshape(NSEQ, CHUNK, PW)
            return piece

        def post_norm(s):
            def piece():
                x1 = xs_ref[rows(s), :] + _rms(y_ref[s], g_post_ref[...])
                y_ref[s] = x1
                lhs2_ref[rows(s), :] = _rms(x1, g_ple_ref[...]).astype(BF16)
            return piece

        def ple(c0):
            def piece():
                gate = _sigmoid(_dot(lhs2_ref[...], _w(w_pg_ref, c0, c0 + PW)))
                pp = _dot(p_ref[...].reshape(TM, D_PLE).astype(BF16), _w(w_pp_ref, c0, c0 + PW))
                y_ref[:, :, c0:c0 + PW] = y_ref[:, :, c0:c0 + PW] + (pp * gate).reshape(NSEQ, CHUNK, PW)
            return piece

        def pre_norm(s):
            def piece():
                hb_ref[rows(s), :] = _rms(x_ref[s], g_pre_ref[...]).astype(BF16)
            return piece

        def in_proj(dst_ref, wc0):
            def make(c0):
                def piece():
                    dst_ref[:, c0:c0 + PW] = _dot(hb_ref[...], _w(w_in_ref, wc0 + c0, wc0 + c0 + PW))
                return piece
            return [make(c0) for c0 in range(0, 2 * GW, PW)]

        def conv_part(s):
            def piece():
                r = rows(s)
                u = pj0_ref[r, 0:GW] * pj0_ref[r, GW:2 * GW]
                uext_ref[s, HIST_ROW:HIST_ROW + CHUNK, :] = u
                cw = conv_w_ref[...]
                ball_ref[r, :] = (cw[0:1, :] * uext_ref[s, HIST_ROW - 2:HIST_ROW - 2 + CHUNK, :]
                                  + cw[1:2, :] * uext_ref[s, HIST_ROW - 1:HIST_ROW - 1 + CHUNK, :]
                                  + cw[2:3, :] * u)
                uext_ref[s, HIST_ROW - 2:HIST_ROW, :] = uext_ref[s, HIST_ROW + CHUNK - 2:HIST_ROW + CHUNK, :]
            return piece

        def gated(s):
            def piece():
                r = rows(s)
                ba = pj1_ref[r, 0:GW]
                za = pj1_ref[r, GW:2 * GW]
                lhs1_ref[r, :] = (za * _sigmoid(za) * ba * ball_ref[r, :]).astype(BF16)
            return piece

        def forget_part(s):
            def piece():
                r = rows(s)
                fg = lb + (1.0 - lb) * _sigmoid(pj0_ref[r, GW:2 * GW])
                pj0_ref[r, GW:2 * GW] = 1.0 - fg
                logf = jnp.log(fg)
                hi = logf.astype(BF16)
                hl_ref[r, :] = hi
                hl_ref[slice(TM + s * CHUNK, TM + (s + 1) * CHUNK), :] = (logf - hi.astype(F32)).astype(BF16)
            return piece

        def cumsum(c0):
            def piece():
                row = lax.broadcasted_iota(jnp.int32, (TM, 2 * TM), 0)
                col = lax.broadcasted_iota(jnp.int32, (TM, 2 * TM), 1) % TM
                ltri = jnp.where((row // CHUNK == col // CHUNK) & (col <= row), 1.0, 0.0).astype(BF16)
                ball_ref[:, c0:c0 + PW] = _dot(ltri, hl_ref[:, c0:c0 + PW])
            return piece

        def decay_part(s):
            def piece():
                r = rows(s)
                b = ball_ref[r, :]
                bmid = b[MID:MID + 1, :]
                blast = b[CHUNK - 1:CHUNK, :]
                q = pj0_ref[r, 0:GW]
                qe_ref[r, :] = (q * _sigmoid(q) * QK_SCALE * jnp.exp(b - bmid)).astype(BF16)
                ke_ref[r, :] = (pj0_ref[r, GW:2 * GW] * jnp.exp(bmid - b)).astype(BF16)
                rows_ref[s, 0:1, :] = jnp.exp(bmid)
                rows_ref[s, 1:2, :] = jnp.exp(blast - bmid)
                rows_ref[s, 2:3, :] = jnp.exp(blast)
            return piece

        def value_part(s):
            def piece():
                r = rows(s)
                iv = pj1_ref[r, 0:GW]
                v_ref[r, :] = iv.astype(BF16)
                vt_ref[s] = iv.T.astype(BF16)
                zb = pj1_ref[r, GW:2 * GW]
                sz_ref[r, :] = zb * _sigmoid(zb)
                xs_ref[r, :] = x_ref[s]
            return piece

        def branch_a_out(c0):
            def piece():
                c = slice(c0, c0 + PW)
                ya = _dot(lhs1_ref[...], _w(w_a_ref, c0, c0 + PW))
                ma_ref[:, c] = _sigmoid(pj0_ref[:, c]) * ya
                sgb_ref[:, c] = _sigmoid(pj0_ref[:, GW + c0:GW + c0 + PW])
            return piece

        _weave([hgrn(s, h) for s in seqs for h in range(H_B)],
               [pre_norm(s) for s in seqs] + in_proj(pj0_ref, C_VC) + in_proj(pj1_ref, C_BZ))
        _weave([branch_b_out(c0) for c0 in cols], [conv_part(s) for s in seqs])
        _weave(in_proj(pj0_ref, C_QF), [gated(s) for s in seqs])
        _weave([out_proj(c0) for c0 in cols], [forget_part(s) for s in seqs])
        _weave(in_proj(pj1_ref, C_IZ) + [cumsum(c0) for c0 in cols], [post_norm(s) for s in seqs])
        _weave([ple(c0) for c0 in cols], [decay_part(s) for s in seqs])
        _weave(in_proj(pj0_ref, C_G) + [branch_a_out(c0) for c0 in cols], [value_part(s) for s in seqs])

        @pl.when(t2 == nt - 1)
        def _store_state():
            for s in range(NSEQ):
                for h in range(H_B):
                    so_ref[s, h] = so_ref[s, h].T

        @pl.when((t1 == nt - 1) & (i < n_tiles))
        def _store_conv():
            convo_ref[...] = uext_ref[:, HIST_ROW - 2:HIST_ROW, :]

    return body


def _run_stream(x, p, conv0, s0, weights, zero_state):
    bsz, tlen, _ = x.shape
    assert bsz % NSEQ == 0 and tlen % CHUNK == 0
    nt = tlen // CHUNK
    n_tiles = (bsz // NSEQ) * nt

    def tile1(i):
        j = jnp.minimum(i, n_tiles - 1)
        return j // nt, j % nt

    def tile2(i):
        j = jnp.maximum(i - 1, 0)
        return j // nt, j % nt

    def resident(arr):
        nd = arr.ndim
        return pl.BlockSpec(arr.shape, lambda i: (0,) * nd, pipeline_mode=pl.Buffered(1))

    in_specs = [
        pl.BlockSpec((NSEQ, CHUNK, D_MODEL), lambda i: (*tile1(i), 0)),
        pl.BlockSpec((NSEQ, CHUNK, D_PLE), lambda i: (*tile2(i), 0)),
        pl.BlockSpec((NSEQ, CONV_W - 1, D_A), lambda i: (tile1(i)[0], 0, 0)),
        (resident(s0) if zero_state else
         pl.BlockSpec((NSEQ, H_B, K_B, V_B), lambda i: (tile2(i)[0], 0, 0, 0), pipeline_mode=pl.Buffered(1))),
    ] + [resident(w) for w in weights]
    out_specs = [
        pl.BlockSpec((NSEQ, CHUNK, D_MODEL), lambda i: (*tile2(i), 0)),
        pl.BlockSpec((NSEQ, CONV_W - 1, D_A), lambda i: (tile1(i)[0], 0, 0)),
        pl.BlockSpec((NSEQ, H_B, K_B, V_B), lambda i: (tile2(i)[0], 0, 0, 0)),
    ]
    out_shape = [
        jax.ShapeDtypeStruct((bsz, tlen, D_MODEL), F32),
        jax.ShapeDtypeStruct((bsz, CONV_W - 1, D_A), F32),
        jax.ShapeDtypeStruct((bsz, H_B, K_B, V_B), F32),
    ]
    scratch = [
        pltpu.VMEM((NSEQ, HIST_ROW + CHUNK, D_A), F32),
        pltpu.VMEM((TM, D_MODEL), BF16),
        pltpu.VMEM((TM, 2 * GW), F32),
        pltpu.VMEM((TM, 2 * GW), F32),
        pltpu.VMEM((TM, D_A), BF16),
        pltpu.VMEM((TM, D_MODEL), BF16),
        pltpu.VMEM((TM, D_MODEL), BF16),
        pltpu.VMEM((TM, D_B), F32),
        pltpu.VMEM((2 * TM, D_B), BF16),
        pltpu.VMEM((TM, D_MODEL), F32),
        pltpu.VMEM((TM, D_MODEL), F32),
        pltpu.VMEM((TM, D_MODEL), F32),
        pltpu.VMEM((TM, D_B), F32),
        pltpu.VMEM((TM, D_B), BF16),
        pltpu.VMEM((TM, D_B), BF16),
        pltpu.VMEM((TM, D_B), BF16),
        pltpu.VMEM((NSEQ, D_B, CHUNK), BF16),
        pltpu.VMEM((NSEQ, 8, D_B), F32),
    ]
    return pl.pallas_call(
        _make_kernel(n_tiles, nt, zero_state),
        grid=(n_tiles + 1,),
        in_specs=in_specs,
        out_specs=out_specs,
        out_shape=out_shape,
        scratch_shapes=scratch,
        compiler_params=pltpu.CompilerParams(
            dimension_semantics=("arbitrary",),
            vmem_limit_bytes=V7X_VMEM_LIMIT_BYTES,
        ),
    )(x, p, conv0, s0, *weights)


def _pack_kernel(w_ref, o_ref):
    o_ref[...] = pltpu.bitcast(w_ref[...].astype(BF16), jnp.uint32)


def _pack_rows(w, col_block_map=None):
    k, n = w.shape
    bk, bn = 256, min(n, GW)
    cmap = col_block_map or (lambda c: c)
    return pl.pallas_call(
        _pack_kernel,
        grid=(k // bk, n // bn),
        in_specs=[pl.BlockSpec((bk, bn), lambda r, c: (r, cmap(c)))],
        out_specs=pl.BlockSpec((bk // 2, bn), lambda r, c: (r, c)),
        out_shape=jax.ShapeDtypeStruct((k // 2, n), jnp.uint32),
    )(w)


def _w_in_block(c):
    assert [SPLIT_NAMES.index(nm) for nm in KERNEL_ORDER] == [0, 2, 1, 3, 4, 5, 6, 7, 8, 9]
    return jnp.where((c == 1) | (c == 2), 3 - c, c)


def kernel(x_prompt, x_sample, state_conv, state_hgrn, p_prompt, p_sample, w_in, conv_w, lb_raw, g_pre,
           g_onorm, w_a_out, w_b_out, w_o, g_post, g_ple, w_ple_gate, w_ple_proj):
    assert w_in.shape[0] == 1 and w_in.shape[2] == N_IN
    i = 0
    weights = (
        _pack_rows(w_in[i], _w_in_block), conv_w[i], lb_raw[i:i + 2], g_pre[i:i + 1], g_onorm[i:i + 1],
        _pack_rows(w_a_out[i]), _pack_rows(w_b_out[i]), _pack_rows(w_o[i]),
        g_post[i:i + 1], g_ple[i:i + 1], _pack_rows(w_ple_gate[i]), _pack_rows(w_ple_proj[i]),
    )
    bp = x_prompt.shape[0]
    zero_conv = jnp.zeros((bp, CONV_W - 1, D_A), F32)
    dummy_s = jnp.zeros((1, 1, 8, V_B), F32)
    yp, cp, sp = _run_stream(x_prompt, p_prompt[i], zero_conv, dummy_s, weights, zero_state=True)
    ys, cs, ss = _run_stream(x_sample, p_sample[i], state_conv[i], state_hgrn[i], weights, zero_state=False)
    return (yp, ys, cp[None], sp[None], cs[None], ss[None])
```

```python
import jax
import jax.numpy as jnp
from jax import lax
from jax.experimental import pallas as pl
from jax.experimental.pallas import tpu as pltpu

D_MODEL = 1024
D_PLE = 256
D_A = 1024
CONV_W = 3
H_B = 8
K_B = 128
V_B = 128
D_B = H_B * V_B
CHUNK = 64
EPS = 1e-6
QK_SCALE = K_B ** -0.5

NSEQ = 4
TM = NSEQ * CHUNK
HIST_ROW = 8
MID = (CHUNK - 1) // 2
PW = 512

SPLIT_NAMES = ("vA", "bA", "cA", "zA", "q", "f", "iv", "zB", "gA", "gB")
KERNEL_ORDER = ("vA", "cA", "bA", "zA", "q", "f", "iv", "zB", "gA", "gB")
GW = 1024
N_IN = GW * len(SPLIT_NAMES)
C_VC, C_BZ, C_QF, C_IZ, C_G = 0, 2 * GW, 4 * GW, 6 * GW, 8 * GW

V7X_VMEM_LIMIT_BYTES = 60 * 1024 * 1024

F32 = jnp.float32
BF16 = jnp.bfloat16


def _dot(a, b):
    return jnp.dot(a, b, preferred_element_type=F32)


def _dot_nt(a, b):
    return lax.dot_general(a, b, (((1,), (1,)), ((), ())), preferred_element_type=F32)


def _w(ref, c0, c1):
    return pltpu.bitcast(ref[:, c0:c1], BF16)


def _sigmoid(x):
    return 1.0 / (1.0 + jnp.exp(-x))


def _rms(x, g):
    return x * lax.rsqrt(jnp.mean(x * x, axis=-1, keepdims=True) + EPS) * g


def _weave(*piece_lists):
    pos = [0] * len(piece_lists)
    for _ in range(sum(len(pl_) for pl_ in piece_lists)):
        k = min((j for j in range(len(piece_lists)) if pos[j] < len(piece_lists[j])),
                key=lambda j: (pos[j] + 0.5) / len(piece_lists[j]))
        piece_lists[k][pos[k]]()
        pos[k] += 1


def _make_kernel(n_tiles, nt, zero_state):
    def body(xn_ref, xl_ref, p_ref, conv0_ref, s0_ref, w_in_ref, conv_w_ref, lb_raw_ref, g_pre_ref,
             g_onorm_ref, w_a_ref, w_b_ref, w_o_ref, g_post_ref, g_ple_ref, w_pg_ref, w_pp_ref,
             y_ref, convo_ref, so_ref,
             uext_ref, hb_ref, pj0_ref, pj1_ref, pj2_ref, lhs1_ref, lhs2_ref, mg_ref, ball_ref, hl_ref,
             ma_ref, sgb_ref, sz_ref, qe_ref, ke_ref, v_ref, vt_ref, rows_ref):
        i = pl.program_id(0)
        j1 = jnp.minimum(i, n_tiles - 1)
        j2 = jnp.maximum(i - 1, 0)
        t1 = j1 % nt
        t2 = j2 % nt

        @pl.when(i == 0)
        def _zero_staging():
            for s in range(NSEQ):
                hb_ref[s * CHUNK:(s + 1) * CHUNK, :] = _rms(xl_ref[s], g_pre_ref[...]).astype(BF16)
            for r in (ma_ref, sgb_ref, sz_ref, qe_ref, ke_ref, v_ref, vt_ref, rows_ref):
                r[...] = jnp.zeros(r.shape, r.dtype)

        @pl.when(t2 == 0)
        def _load_state():
            for s in range(NSEQ):
                for h in range(H_B):
                    if zero_state:
                        so_ref[s, h] = jnp.zeros((V_B, K_B), F32)
                    else:
                        so_ref[s, h] = s0_ref[s, h].T

        @pl.when(t1 == 0)
        def _load_conv():
            uext_ref[:, HIST_ROW - 2:HIST_ROW, :] = conv0_ref[...]

        ti = lax.broadcasted_iota(jnp.int32, (CHUNK, CHUNK), 0)
        si = lax.broadcasted_iota(jnp.int32, (CHUNK, CHUNK), 1)
        causal = si <= ti
        lbr = lb_raw_ref[...]
        lbe = jnp.exp(lbr - jnp.max(lbr, axis=0, keepdims=True))
        lb = lbe[0:1, :] / jnp.sum(lbe, axis=0, keepdims=True)
        seqs = range(NSEQ)
        cols = range(0, D_MODEL, PW)

        def rows(s):
            return slice(s * CHUNK, (s + 1) * CHUNK)

        def hgrn(s, h):
            def piece():
                r, c = rows(s), slice(h * K_B, (h + 1) * K_B)
                qe = qe_ref[r, c]
                ke = ke_ref[r, c]
                att = jnp.where(causal, _dot_nt(qe, ke), 0.0).astype(BF16)
                st = so_ref[s, h]
                e_mid = rows_ref[s, 0:1, c]
                e_last = rows_ref[s, 1:2, c]
                d_last = rows_ref[s, 2:3, c]
                o = _dot(att, v_ref[r, c]) + _dot_nt(qe, (st * e_mid).astype(BF16))
                so_ref[s, h] = st * d_last + _dot(vt_ref[s, c, :], ke) * e_last
                on = o * lax.rsqrt(jnp.mean(o * o, axis=-1, keepdims=True) + EPS) * g_onorm_ref[:, c]
                lhs2_ref[r, c] = (on * sz_ref[r, c]).astype(BF16)
            return piece

        def branch_b_out(c0):
            def piece():
                c = slice(c0, c0 + PW)
                yb = _dot(lhs2_ref[...], _w(w_b_ref, c0, c0 + PW))
                mg_ref[:, c] = (ma_ref[:, c] + sgb_ref[:, c] * yb).astype(BF16)
            return piece

        def out_proj(c0):
            def piece():
                out = _dot(mg_ref[...], _w(w_o_ref, c0, c0 + PW))
                y_ref[:, :, c0:c0 + PW] = out.reshape(NSEQ, CHUNK, PW)
            return piece

        def post_norm(s):
            def piece():
                x1 = xl_ref[s] + _rms(y_ref[s], g_post_ref[...])
                y_ref[s] = x1
                lhs2_ref[rows(s), :] = _rms(x1, g_ple_ref[...]).astype(BF16)
            return piece

        def ple(c0):
            def piece():
                gate = _sigmoid(_dot(lhs2_ref[...], _w(w_pg_ref, c0, c0 + PW)))
                pp = _dot(p_ref[...].reshape(TM, D_PLE).astype(BF16), _w(w_pp_ref, c0, c0 + PW))
                y_ref[:, :, c0:c0 + PW] = y_ref[:, :, c0:c0 + PW] + (pp * gate).reshape(NSEQ, CHUNK, PW)
            return piece

        def pre_norm_next(s):
            def piece():
                hb_ref[rows(s), :] = _rms(xn_ref[s], g_pre_ref[...]).astype(BF16)
            return piece

        def in_proj(dst_ref, wc0):
            def make(c0):
                def piece():
                    dst_ref[:, c0:c0 + PW] = _dot(hb_ref[...], _w(w_in_ref, wc0 + c0, wc0 + c0 + PW))
                return piece
            return [make(c0) for c0 in range(0, 2 * GW, PW)]

        def conv_part(s):
            def piece():
                r = rows(s)
                u = pj0_ref[r, 0:GW] * pj0_ref[r, GW:2 * GW]
                uext_ref[s, HIST_ROW:HIST_ROW + CHUNK, :] = u
                cw = conv_w_ref[...]
                ball_ref[r, :] = (cw[0:1, :] * uext_ref[s, HIST_ROW - 2:HIST_ROW - 2 + CHUNK, :]
                                  + cw[1:2, :] * uext_ref[s, HIST_ROW - 1:HIST_ROW - 1 + CHUNK, :]
                                  + cw[2:3, :] * u)
                uext_ref[s, HIST_ROW - 2:HIST_ROW, :] = uext_ref[s, HIST_ROW + CHUNK - 2:HIST_ROW + CHUNK, :]
            return piece

        def gated(s):
            def piece():
                r = rows(s)
                ba = pj1_ref[r, 0:GW]
                za = pj1_ref[r, GW:2 * GW]
                lhs1_ref[r, :] = (za * _sigmoid(za) * ba * ball_ref[r, :]).astype(BF16)
            return piece

        def forget_part(s):
            def piece():
                r = rows(s)
                fg = lb + (1.0 - lb) * _sigmoid(pj0_ref[r, GW:2 * GW])
                pj0_ref[r, GW:2 * GW] = 1.0 - fg
                logf = jnp.log(fg)
                hi = logf.astype(BF16)
                hl_ref[r, :] = hi
                hl_ref[slice(TM + s * CHUNK, TM + (s + 1) * CHUNK), :] = (logf - hi.astype(F32)).astype(BF16)
            return piece

        def cumsum(c0):
            def piece():
                row = lax.broadcasted_iota(jnp.int32, (TM, 2 * TM), 0)
                col = lax.broadcasted_iota(jnp.int32, (TM, 2 * TM), 1) % TM
                ltri = jnp.where((row // CHUNK == col // CHUNK) & (col <= row), 1.0, 0.0).astype(BF16)
                ball_ref[:, c0:c0 + PW] = _dot(ltri, hl_ref[:, c0:c0 + PW])
            return piece

        def decay_part(s):
            def piece():
                r = rows(s)
                b = ball_ref[r, :]
                bmid = b[MID:MID + 1, :]
                blast = b[CHUNK - 1:CHUNK, :]
                q = pj0_ref[r, 0:GW]
                qe_ref[r, :] = (q * _sigmoid(q) * QK_SCALE * jnp.exp(b - bmid)).astype(BF16)
                ke_ref[r, :] = (pj0_ref[r, GW:2 * GW] * jnp.exp(bmid - b)).astype(BF16)
                rows_ref[s, 0:1, :] = jnp.exp(bmid)
                rows_ref[s, 1:2, :] = jnp.exp(blast - bmid)
                rows_ref[s, 2:3, :] = jnp.exp(blast)
            return piece

        def value_part(s):
            def piece():
                r = rows(s)
                iv = pj1_ref[r, 0:GW]
                v_ref[r, :] = iv.astype(BF16)
                vt_ref[s] = iv.T.astype(BF16)
                zb = pj1_ref[r, GW:2 * GW]
                sz_ref[r, :] = zb * _sigmoid(zb)
            return piece

        def branch_a_out(c0):
            def piece():
                c = slice(c0, c0 + PW)
                ya = _dot(lhs1_ref[...], _w(w_a_ref, c0, c0 + PW))
                ma_ref[:, c] = _sigmoid(pj2_ref[:, c]) * ya
            return piece

        def gate_b(s):
            def piece():
                sgb_ref[rows(s), :] = _sigmoid(pj2_ref[rows(s), GW:2 * GW])
            return piece

        _weave([hgrn(s, h) for s in seqs for h in range(H_B)],
               in_proj(pj0_ref, C_VC) + in_proj(pj1_ref, C_BZ))
        _weave([branch_b_out(c0) for c0 in cols], [conv_part(s) for s in seqs])
        _weave(in_proj(pj0_ref, C_QF), [gated(s) for s in seqs])
        _weave([out_proj(c0) for c0 in cols] + in_proj(pj2_ref, C_G), [forget_part(s) for s in seqs])
        _weave(in_proj(pj1_ref, C_IZ) + [cumsum(c0) for c0 in cols],
               [post_norm(s) for s in seqs] + [gate_b(s) for s in seqs])
        _weave([ple(c0) for c0 in cols], [decay_part(s) for s in seqs])
        _weave([branch_a_out(c0) for c0 in cols],
               [value_part(s) for s in seqs] + [pre_norm_next(s) for s in seqs])

        @pl.when(t2 == nt - 1)
        def _store_state():
            for s in range(NSEQ):
                for h in range(H_B):
                    so_ref[s, h] = so_ref[s, h].T

        @pl.when((t1 == nt - 1) & (i < n_tiles))
        def _store_conv():
            convo_ref[...] = uext_ref[:, HIST_ROW - 2:HIST_ROW, :]

    return body


def _run_stream(x, p, conv0, s0, weights, zero_state):
    bsz, tlen, _ = x.shape
    assert bsz % NSEQ == 0 and tlen % CHUNK == 0
    nt = tlen // CHUNK
    n_tiles = (bsz // NSEQ) * nt

    def tile1(i):
        j = jnp.minimum(i, n_tiles - 1)
        return j // nt, j % nt

    def tile2(i):
        j = jnp.maximum(i - 1, 0)
        return j // nt, j % nt

    def tile_next(i):
        j = jnp.minimum(i + 1, n_tiles - 1)
        return j // nt, j % nt

    def resident(arr):
        nd = arr.ndim
        return pl.BlockSpec(arr.shape, lambda i: (0,) * nd, pipeline_mode=pl.Buffered(1))

    in_specs = [
        pl.BlockSpec((NSEQ, CHUNK, D_MODEL), lambda i: (*tile_next(i), 0)),
        pl.BlockSpec((NSEQ, CHUNK, D_MODEL), lambda i: (*tile2(i), 0)),
        pl.BlockSpec((NSEQ, CHUNK, D_PLE), lambda i: (*tile2(i), 0)),
        pl.BlockSpec((NSEQ, CONV_W - 1, D_A), lambda i: (tile1(i)[0], 0, 0)),
        (resident(s0) if zero_state else
         pl.BlockSpec((NSEQ, H_B, K_B, V_B), lambda i: (tile2(i)[0], 0, 0, 0), pipeline_mode=pl.Buffered(1))),
    ] + [resident(w) for w in weights]
    out_specs = [
        pl.BlockSpec((NSEQ, CHUNK, D_MODEL), lambda i: (*tile2(i), 0)),
        pl.BlockSpec((NSEQ, CONV_W - 1, D_A), lambda i: (tile1(i)[0], 0, 0)),
        pl.BlockSpec((NSEQ, H_B, K_B, V_B), lambda i: (tile2(i)[0], 0, 0, 0)),
    ]
    out_shape = [
        jax.ShapeDtypeStruct((bsz, tlen, D_MODEL), F32),
        jax.ShapeDtypeStruct((bsz, CONV_W - 1, D_A), F32),
        jax.ShapeDtypeStruct((bsz, H_B, K_B, V_B), F32),
    ]
    scratch = [
        pltpu.VMEM((NSEQ, HIST_ROW + CHUNK, D_A), F32),
        pltpu.VMEM((TM, D_MODEL), BF16),
        pltpu.VMEM((TM, 2 * GW), F32),
        pltpu.VMEM((TM, 2 * GW), F32),
        pltpu.VMEM((TM, 2 * GW), F32),
        pltpu.VMEM((TM, D_A), BF16),
        pltpu.VMEM((TM, D_MODEL), BF16),
        pltpu.VMEM((TM, D_MODEL), BF16),
        pltpu.VMEM((TM, D_B), F32),
        pltpu.VMEM((2 * TM, D_B), BF16),
        pltpu.VMEM((TM, D_MODEL), F32),
        pltpu.VMEM((TM, D_MODEL), F32),
        pltpu.VMEM((TM, D_B), F32),
        pltpu.VMEM((TM, D_B), BF16),
        pltpu.VMEM((TM, D_B), BF16),
        pltpu.VMEM((TM, D_B), BF16),
        pltpu.VMEM((NSEQ, D_B, CHUNK), BF16),
        pltpu.VMEM((NSEQ, 8, D_B), F32),
    ]
    return pl.pallas_call(
        _make_kernel(n_tiles, nt, zero_state),
        grid=(n_tiles + 1,),
        in_specs=in_specs,
        out_specs=out_specs,
        out_shape=out_shape,
        scratch_shapes=scratch,
        compiler_params=pltpu.CompilerParams(
            dimension_semantics=("arbitrary",),
            vmem_limit_bytes=V7X_VMEM_LIMIT_BYTES,
        ),
    )(x, x, p, conv0, s0, *weights)


def _pack_kernel(w_ref, o_ref):
    o_ref[...] = pltpu.bitcast(w_ref[...].astype(BF16), jnp.uint32)


def _pack_rows(w, col_block_map=None):
    k, n = w.shape
    bk, bn = 256, min(n, GW)
    cmap = col_block_map or (lambda c: c)
    return pl.pallas_call(
        _pack_kernel,
        grid=(k // bk, n // bn),
        in_specs=[pl.BlockSpec((bk, bn), lambda r, c: (r, cmap(c)))],
        out_specs=pl.BlockSpec((bk // 2, bn), lambda r, c: (r, c)),
        out_shape=jax.ShapeDtypeStruct((k // 2, n), jnp.uint32),
    )(w)


def _w_in_block(c):
    assert [SPLIT_NAMES.index(nm) for nm in KERNEL_ORDER] == [0, 2, 1, 3, 4, 5, 6, 7, 8, 9]
    return jnp.where((c == 1) | (c == 2), 3 - c, c)


def kernel(x_prompt, x_sample, state_conv, state_hgrn, p_prompt, p_sample, w_in, conv_w, lb_raw, g_pre,
           g_onorm, w_a_out, w_b_out, w_o, g_post, g_ple, w_ple_gate, w_ple_proj):
    assert w_in.shape[0] == 1 and w_in.shape[2] == N_IN
    i = 0
    weights = (
        _pack_rows(w_in[i], _w_in_block), conv_w[i], lb_raw[i:i + 2], g_pre[i:i + 1], g_onorm[i:i + 1],
        _pack_rows(w_a_out[i]), _pack_rows(w_b_out[i]), _pack_rows(w_o[i]),
        g_post[i:i + 1], g_ple[i:i + 1], _pack_rows(w_ple_gate[i]), _pack_rows(w_ple_proj[i]),
    )
    bp = x_prompt.shape[0]
    zero_conv = jnp.zeros((bp, CONV_W - 1, D_A), F32)
    dummy_s = jnp.zeros((1, 1, 8, V_B), F32)
    yp, cp, sp = _run_stream(x_prompt, p_prompt[i], zero_conv, dummy_s, weights, zero_state=True)
    ys, cs, ss = _run_stream(x_sample, p_sample[i], state_conv[i], state_hgrn[i], weights, zero_state=False)
    return (yp, ys, cp[None], sp[None], cs[None], ss[None])
```

```python
import jax
import jax.numpy as jnp
from jax import lax
from jax.experimental import pallas as pl
from jax.experimental.pallas import tpu as pltpu

D_MODEL = 1024
D_PLE = 256
D_A = 1024
CONV_W = 3
H_B = 8
K_B = 128
V_B = 128
D_B = H_B * V_B
CHUNK = 64
EPS = 1e-6
QK_SCALE = K_B ** -0.5

NSEQ = 4
TM = NSEQ * CHUNK
HIST_ROW = 8
MID = (CHUNK - 1) // 2
PW = 512

SPLIT_NAMES = ("vA", "bA", "cA", "zA", "q", "f", "iv", "zB", "gA", "gB")
KERNEL_ORDER = ("vA", "cA", "bA", "zA", "q", "f", "iv", "zB", "gA", "gB")
GW = 1024
N_IN = GW * len(SPLIT_NAMES)
C_VC, C_BZ, C_QF, C_IZ, C_G = 0, 2 * GW, 4 * GW, 6 * GW, 8 * GW

V7X_VMEM_LIMIT_BYTES = 60 * 1024 * 1024

F32 = jnp.float32
BF16 = jnp.bfloat16


def _dot(a, b):
    return jnp.dot(a, b, preferred_element_type=F32)


def _dot_nt(a, b):
    return lax.dot_general(a, b, (((1,), (1,)), ((), ())), preferred_element_type=F32)


def _w(ref, c0, c1):
    return pltpu.bitcast(ref[:, c0:c1], BF16)


def _sigmoid(x):
    return 1.0 / (1.0 + jnp.exp(-x))


def _rms(x, g):
    return x * lax.rsqrt(jnp.mean(x * x, axis=-1, keepdims=True) + EPS) * g


def _weave(*piece_lists):
    pos = [0] * len(piece_lists)
    for _ in range(sum(len(pl_) for pl_ in piece_lists)):
        k = min((j for j in range(len(piece_lists)) if pos[j] < len(piece_lists[j])),
                key=lambda j: (pos[j] + 0.5) / len(piece_lists[j]))
        piece_lists[k][pos[k]]()
        pos[k] += 1


def _make_kernel(n_tiles, nt, zero_state):
    def body(x_ref, p_ref, conv0_ref, s0_ref, w_in_ref, conv_w_ref, lb_raw_ref, g_pre_ref,
             g_onorm_ref, w_a_ref, w_b_ref, w_o_ref, g_post_ref, g_ple_ref, w_pg_ref, w_pp_ref,
             y_ref, convo_ref, so_ref,
             uext_ref, hb_ref, pj0_ref, pj1_ref, lhs1_ref, lhs2_ref, mg_ref, ball_ref, hl_ref,
             xs_ref, ma_ref, sgb_ref, sz_ref, qe_ref, qb_ref, ke_ref, v_ref, klt_ref, dl_ref):
        i = pl.program_id(0)
        j1 = jnp.minimum(i, n_tiles - 1)
        j2 = jnp.maximum(i - 1, 0)
        t1 = j1 % nt
        t2 = j2 % nt

        @pl.when(i == 0)
        def _zero_staging():
            for r in (xs_ref, ma_ref, sgb_ref, sz_ref, qe_ref, qb_ref, ke_ref, v_ref, klt_ref, dl_ref):
                r[...] = jnp.zeros(r.shape, r.dtype)

        @pl.when(t2 == 0)
        def _load_state():
            if zero_state:
                so_ref[...] = jnp.zeros(so_ref.shape, F32)
            else:
                so_ref[...] = s0_ref[...]

        @pl.when(t1 == 0)
        def _load_conv():
            uext_ref[:, HIST_ROW - 2:HIST_ROW, :] = conv0_ref[...]

        ti = lax.broadcasted_iota(jnp.int32, (CHUNK, CHUNK), 0)
        si = lax.broadcasted_iota(jnp.int32, (CHUNK, CHUNK), 1)
        causal = si <= ti
        lbr = lb_raw_ref[...]
        lbe = jnp.exp(lbr - jnp.max(lbr, axis=0, keepdims=True))
        lb = lbe[0:1, :] / jnp.sum(lbe, axis=0, keepdims=True)
        seqs = range(NSEQ)
        cols = range(0, D_MODEL, PW)

        def rows(s):
            return slice(s * CHUNK, (s + 1) * CHUNK)

        def hgrn(s, h):
            def piece():
                r, c = rows(s), slice(h * K_B, (h + 1) * K_B)
                qe = qe_ref[r, c]
                att = jnp.where(causal, _dot_nt(qe, ke_ref[r, c]), 0.0).astype(BF16)
                av = _dot(jnp.concatenate([att, klt_ref[s, c, :]], axis=0), v_ref[r, c])
                st = so_ref[s, h]
                o = av[0:CHUNK] + _dot(qb_ref[r, c], st.astype(BF16))
                decay = jnp.broadcast_to(dl_ref[s:s + 1, c], (V_B, K_B)).T
                so_ref[s, h] = st * decay + av[CHUNK:CHUNK + K_B]
                on = o * lax.rsqrt(jnp.mean(o * o, axis=-1, keepdims=True) + EPS) * g_onorm_ref[:, c]
                lhs2_ref[r, c] = (on * sz_ref[r, c]).astype(BF16)
            return piece

        def branch_b_out(c0):
            def piece():
                c = slice(c0, c0 + PW)
                yb = _dot(lhs2_ref[...], _w(w_b_ref, c0, c0 + PW))
                mg_ref[:, c] = (ma_ref[:, c] + sgb_ref[:, c] * yb).astype(BF16)
            return piece

        def out_proj(c0):
            def piece():
                out = _dot(mg_ref[...], _w(w_o_ref, c0, c0 + PW))
                y_ref[:, :, c0:c0 + PW] = out.reshape(NSEQ, CHUNK, PW)
            return piece

        def post_norm(s):
            def piece():
                x1 = xs_ref[rows(s), :] + _rms(y_ref[s], g_post_ref[...])
                y_ref[s] = x1
                lhs2_ref[rows(s), :] = _rms(x1, g_ple_ref[...]).astype(BF16)
            return piece

        def ple(c0):
            def piece():
                gate = _sigmoid(_dot(lhs2_ref[...], _w(w_pg_ref, c0, c0 + PW)))
                pp = _dot(p_ref[...].reshape(TM, D_PLE).astype(BF16), _w(w_pp_ref, c0, c0 + PW))
                y_ref[:, :, c0:c0 + PW] = y_ref[:, :, c0:c0 + PW] + (pp * gate).reshape(NSEQ, CHUNK, PW)
            return piece

        def pre_norm(s):
            def piece():
                hb_ref[rows(s), :] = _rms(x_ref[s], g_pre_ref[...]).astype(BF16)
            return piece

        def in_proj(dst_ref, wc0):
            def make(c0):
                def piece():
                    dst_ref[:, c0:c0 + PW] = _dot(hb_ref[...], _w(w_in_ref, wc0 + c0, wc0 + c0 + PW))
                return piece
            return [make(c0) for c0 in range(0, 2 * GW, PW)]

        def conv_part(s):
            def piece():
                r = rows(s)
                u = pj0_ref[r, 0:GW] * pj0_ref[r, GW:2 * GW]
                uext_ref[s, HIST_ROW:HIST_ROW + CHUNK, :] = u
                cw = conv_w_ref[...]
                ball_ref[r, :] = (cw[0:1, :] * uext_ref[s, HIST_ROW - 2:HIST_ROW - 2 + CHUNK, :]
                                  + cw[1:2, :] * uext_ref[s, HIST_ROW - 1:HIST_ROW - 1 + CHUNK, :]
                                  + cw[2:3, :] * u)
                uext_ref[s, HIST_ROW - 2:HIST_ROW, :] = uext_ref[s, HIST_ROW + CHUNK - 2:HIST_ROW + CHUNK, :]
            return piece

        def gated(s):
            def piece():
                r = rows(s)
                ba = pj1_ref[r, 0:GW]
                za = pj1_ref[r, GW:2 * GW]
                lhs1_ref[r, :] = (za * _sigmoid(za) * ba * ball_ref[r, :]).astype(BF16)
            return piece

        def forget_part(s):
            def piece():
                r = rows(s)
                fg = lb + (1.0 - lb) * _sigmoid(pj0_ref[r, GW:2 * GW])
                pj0_ref[r, GW:2 * GW] = 1.0 - fg
                logf = jnp.log(fg)
                hi = logf.astype(BF16)
                hl_ref[r, :] = hi
                hl_ref[slice(TM + s * CHUNK, TM + (s + 1) * CHUNK), :] = (logf - hi.astype(F32)).astype(BF16)
            return piece

        def cumsum(c0):
            def piece():
                row = lax.broadcasted_iota(jnp.int32, (TM, 2 * TM), 0)
                col = lax.broadcasted_iota(jnp.int32, (TM, 2 * TM), 1) % TM
                ltri = jnp.where((row // CHUNK == col // CHUNK) & (col <= row), 1.0, 0.0).astype(BF16)
                ball_ref[:, c0:c0 + PW] = _dot(ltri, hl_ref[:, c0:c0 + PW])
            return piece

        def decay_part(s):
            def piece():
                r = rows(s)
                b = ball_ref[r, :]
                bmid = b[MID:MID + 1, :]
                blast = b[CHUNK - 1:CHUNK, :]
                q = pj0_ref[r, 0:GW]
                qe = q * _sigmoid(q) * QK_SCALE * jnp.exp(b - bmid)
                qe_ref[r, :] = qe.astype(BF16)
                qb_ref[r, :] = (qe * jnp.exp(bmid)).astype(BF16)
                ke = pj0_ref[r, GW:2 * GW] * jnp.exp(bmid - b)
                ke_ref[r, :] = ke.astype(BF16)
                klt_ref[s] = (ke * jnp.exp(blast - bmid)).T.astype(BF16)
                dl_ref[s:s + 1, :] = jnp.exp(blast)
            return piece

        def value_part(s):
            def piece():
                r = rows(s)
                v_ref[r, :] = pj1_ref[r, 0:GW].astype(BF16)
                zb = pj1_ref[r, GW:2 * GW]
                sz_ref[r, :] = zb * _sigmoid(zb)
                xs_ref[r, :] = x_ref[s]
            return piece

        def branch_a_out(c0):
            def piece():
                c = slice(c0, c0 + PW)
                ya = _dot(lhs1_ref[...], _w(w_a_ref, c0, c0 + PW))
                ma_ref[:, c] = _sigmoid(pj0_ref[:, c]) * ya
                sgb_ref[:, c] = _sigmoid(pj0_ref[:, GW + c0:GW + c0 + PW])
            return piece

        _weave([hgrn(s, h) for s in seqs for h in range(H_B)],
               [pre_norm(s) for s in seqs] + in_proj(pj0_ref, C_VC) + in_proj(pj1_ref, C_BZ))
        _weave([branch_b_out(c0) for c0 in cols], [conv_part(s) for s in seqs])
        _weave(in_proj(pj0_ref, C_QF), [gated(s) for s in seqs])
        _weave([out_proj(c0) for c0 in cols], [forget_part(s) for s in seqs])
        _weave(in_proj(pj1_ref, C_IZ) + [cumsum(c0) for c0 in cols], [post_norm(s) for s in seqs])
        _weave([ple(c0) for c0 in cols], [decay_part(s) for s in seqs])
        _weave(in_proj(pj0_ref, C_G) + [branch_a_out(c0) for c0 in cols], [value_part(s) for s in seqs])

        @pl.when((t1 == nt - 1) & (i < n_tiles))
        def _store_conv():
            convo_ref[...] = uext_ref[:, HIST_ROW - 2:HIST_ROW, :]

    return body


def _run_stream(x, p, conv0, s0, weights, zero_state):
    bsz, tlen, _ = x.shape
    assert bsz % NSEQ == 0 and tlen % CHUNK == 0
    nt = tlen // CHUNK
    n_tiles = (bsz // NSEQ) * nt

    def tile1(i):
        j = jnp.minimum(i, n_tiles - 1)
        return j // nt, j % nt

    def tile2(i):
        j = jnp.maximum(i - 1, 0)
        return j // nt, j % nt

    def resident(arr):
        nd = arr.ndim
        return pl.BlockSpec(arr.shape, lambda i: (0,) * nd, pipeline_mode=pl.Buffered(1))

    in_specs = [
        pl.BlockSpec((NSEQ, CHUNK, D_MODEL), lambda i: (*tile1(i), 0)),
        pl.BlockSpec((NSEQ, CHUNK, D_PLE), lambda i: (*tile2(i), 0)),
        pl.BlockSpec((NSEQ, CONV_W - 1, D_A), lambda i: (tile1(i)[0], 0, 0)),
        (resident(s0) if zero_state else
         pl.BlockSpec((NSEQ, H_B, K_B, V_B), lambda i: (tile2(i)[0], 0, 0, 0), pipeline_mode=pl.Buffered(1))),
    ] + [resident(w) for w in weights]
    out_specs = [
        pl.BlockSpec((NSEQ, CHUNK, D_MODEL), lambda i: (*tile2(i), 0)),
        pl.BlockSpec((NSEQ, CONV_W - 1, D_A), lambda i: (tile1(i)[0], 0, 0)),
        pl.BlockSpec((NSEQ, H_B, K_B, V_B), lambda i: (tile2(i)[0], 0, 0, 0)),
    ]
    out_shape = [
        jax.ShapeDtypeStruct((bsz, tlen, D_MODEL), F32),
        jax.ShapeDtypeStruct((bsz, CONV_W - 1, D_A), F32),
        jax.ShapeDtypeStruct((bsz, H_B, K_B, V_B), F32),
    ]
    scratch = [
        pltpu.VMEM((NSEQ, HIST_ROW + CHUNK, D_A), F32),
        pltpu.VMEM((TM, D_MODEL), BF16),
        pltpu.VMEM((TM, 2 * GW), F32),
        pltpu.VMEM((TM, 2 * GW), F32),
        pltpu.VMEM((TM, D_A), BF16),
        pltpu.VMEM((TM, D_MODEL), BF16),
        pltpu.VMEM((TM, D_MODEL), BF16),
        pltpu.VMEM((TM, D_B), F32),
        pltpu.VMEM((2 * TM, D_B), BF16),
        pltpu.VMEM((TM, D_MODEL), F32),
        pltpu.VMEM((TM, D_MODEL), F32),
        pltpu.VMEM((TM, D_MODEL), F32),
        pltpu.VMEM((TM, D_B), F32),
        pltpu.VMEM((TM, D_B), BF16),
        pltpu.VMEM((TM, D_B), BF16),
        pltpu.VMEM((TM, D_B), BF16),
        pltpu.VMEM((TM, D_B), BF16),
        pltpu.VMEM((NSEQ, D_B, CHUNK), BF16),
        pltpu.VMEM((8, D_B), F32),
    ]
    return pl.pallas_call(
        _make_kernel(n_tiles, nt, zero_state),
        grid=(n_tiles + 1,),
        in_specs=in_specs,
        out_specs=out_specs,
        out_shape=out_shape,
        scratch_shapes=scratch,
        compiler_params=pltpu.CompilerParams(
            dimension_semantics=("arbitrary",),
            vmem_limit_bytes=V7X_VMEM_LIMIT_BYTES,
        ),
    )(x, p, conv0, s0, *weights)


def _pack_kernel(w_ref, o_ref):
    o_ref[...] = pltpu.bitcast(w_ref[...].astype(BF16), jnp.uint32)


def _pack_rows(w, col_block_map=None):
    k, n = w.shape
    bk, bn = 256, min(n, GW)
    cmap = col_block_map or (lambda c: c)
    return pl.pallas_call(
        _pack_kernel,
        grid=(k // bk, n // bn),
        in_specs=[pl.BlockSpec((bk, bn), lambda r, c: (r, cmap(c)))],
        out_specs=pl.BlockSpec((bk // 2, bn), lambda r, c: (r, c)),
        out_shape=jax.ShapeDtypeStruct((k // 2, n), jnp.uint32),
    )(w)


def _w_in_block(c):
    assert [SPLIT_NAMES.index(nm) for nm in KERNEL_ORDER] == [0, 2, 1, 3, 4, 5, 6, 7, 8, 9]
    return jnp.where((c == 1) | (c == 2), 3 - c, c)


def kernel(x_prompt, x_sample, state_conv, state_hgrn, p_prompt, p_sample, w_in, conv_w, lb_raw, g_pre,
           g_onorm, w_a_out, w_b_out, w_o, g_post, g_ple, w_ple_gate, w_ple_proj):
    assert w_in.shape[0] == 1 and w_in.shape[2] == N_IN
    i = 0
    weights = (
        _pack_rows(w_in[i], _w_in_block), conv_w[i], lb_raw[i:i + 2], g_pre[i:i + 1], g_onorm[i:i + 1],
        _pack_rows(w_a_out[i]), _pack_rows(w_b_out[i]), _pack_rows(w_o[i]),
        g_post[i:i + 1], g_ple[i:i + 1], _pack_rows(w_ple_gate[i]), _pack_rows(w_ple_proj[i]),
    )
    bp = x_prompt.shape[0]
    zero_conv = jnp.zeros((bp, CONV_W - 1, D_A), F32)
    dummy_s = jnp.zeros((1, 1, 8, V_B), F32)
    yp, cp, sp = _run_stream(x_prompt, p_prompt[i], zero_conv, dummy_s, weights, zero_state=True)
    ys, cs, ss = _run_stream(x_sample, p_sample[i], state_conv[i], state_hgrn[i], weights, zero_state=False)
    return (yp, ys, cp[None], sp[None], cs[None], ss[None])
```

```python
import jax
import jax.numpy as jnp
from jax import lax
from jax.experimental import pallas as pl
from jax.experimental.pallas import tpu as pltpu

D_MODEL = 1024
D_PLE = 256
D_A = 1024
CONV_W = 3
H_B = 8
K_B = 128
V_B = 128
D_B = H_B * V_B
CHUNK = 64
EPS = 1e-6
QK_SCALE = K_B ** -0.5

NSEQ = 4
TM = NSEQ * CHUNK
HIST_ROW = 8
MID = (CHUNK - 1) // 2
PW = 512

SPLIT_NAMES = ("vA", "bA", "cA", "zA", "q", "f", "iv", "zB", "gA", "gB")
KERNEL_ORDER = ("vA", "cA", "bA", "zA", "q", "f", "iv", "zB", "gA", "gB")
GW = 1024
N_IN = GW * len(SPLIT_NAMES)
C_VC, C_BZ, C_QF, C_IZ, C_G = 0, 2 * GW, 4 * GW, 6 * GW, 8 * GW

V7X_VMEM_LIMIT_BYTES = 60 * 1024 * 1024

F32 = jnp.float32
BF16 = jnp.bfloat16


def _dot(a, b):
    return jnp.dot(a, b, preferred_element_type=F32)


def _dot_nt(a, b):
    return lax.dot_general(a, b, (((1,), (1,)), ((), ())), preferred_element_type=F32)


def _w(ref, c0, c1):
    return pltpu.bitcast(ref[:, c0:c1], BF16)


def _sigmoid(x):
    return 1.0 / (1.0 + jnp.exp(-x))


def _rms(x, g):
    return x * lax.rsqrt(jnp.mean(x * x, axis=-1, keepdims=True) + EPS) * g


def _weave(*piece_lists):
    pos = [0] * len(piece_lists)
    for _ in range(sum(len(pl_) for pl_ in piece_lists)):
        k = min((j for j in range(len(piece_lists)) if pos[j] < len(piece_lists[j])),
                key=lambda j: (pos[j] + 0.5) / len(piece_lists[j]))
        piece_lists[k][pos[k]]()
        pos[k] += 1


def _make_kernel(n_tiles, nt, zero_state):
    def body(x_ref, p_ref, conv0_ref, s0_ref, w_in_ref, conv_w_ref, lb_raw_ref, g_pre_ref,
             g_onorm_ref, w_a_ref, w_b_ref, w_o_ref, g_post_ref, g_ple_ref, w_pg_ref, w_pp_ref,
             y_ref, convo_ref, so_ref,
             uext_ref, hb_ref, pj0_ref, pj1_ref, lhs1_ref, lhs2_ref, mg_ref, ball_ref, hl_ref, att_ref,
             xs_ref, ma_ref, sgb_ref, sz_ref, qe_ref, ke_ref, v_ref, vt_ref, rows_ref):
        i = pl.program_id(0)
        j1 = jnp.minimum(i, n_tiles - 1)
        j2 = jnp.maximum(i - 1, 0)
        t1 = j1 % nt
        t2 = j2 % nt

        @pl.when(i == 0)
        def _zero_staging():
            for r in (xs_ref, ma_ref, sgb_ref, sz_ref, qe_ref, ke_ref, v_ref, vt_ref, rows_ref):
                r[...] = jnp.zeros(r.shape, r.dtype)

        @pl.when(t2 == 0)
        def _load_state():
            for s in range(NSEQ):
                for h in range(H_B):
                    if zero_state:
                        so_ref[s, h] = jnp.zeros((V_B, K_B), F32)
                    else:
                        so_ref[s, h] = s0_ref[s, h].T

        @pl.when(t1 == 0)
        def _load_conv():
            uext_ref[:, HIST_ROW - 2:HIST_ROW, :] = conv0_ref[...]

        ti = lax.broadcasted_iota(jnp.int32, (CHUNK, CHUNK), 0)
        si = lax.broadcasted_iota(jnp.int32, (CHUNK, CHUNK), 1)
        causal = si <= ti
        lbr = lb_raw_ref[...]
        lbe = jnp.exp(lbr - jnp.max(lbr, axis=0, keepdims=True))
        lb = lbe[0:1, :] / jnp.sum(lbe, axis=0, keepdims=True)
        seqs = range(NSEQ)
        heads = range(H_B)
        cols = range(0, D_MODEL, PW)

        def rows(s):
            return slice(s * CHUNK, (s + 1) * CHUNK)

        def scores(s, h):
            def piece():
                r, c = rows(s), slice(h * K_B, (h + 1) * K_B)
                att = jnp.where(causal, _dot_nt(qe_ref[r, c], ke_ref[r, c]), 0.0)
                att_ref[r, h * CHUNK:(h + 1) * CHUNK] = att.astype(BF16)
            return piece

        def hgrn(s, h):
            def piece():
                r, c = rows(s), slice(h * K_B, (h + 1) * K_B)
                st = so_ref[s, h]
                e_mid = rows_ref[s, 0:1, c]
                e_last = rows_ref[s, 1:2, c]
                d_last = rows_ref[s, 2:3, c]
                o = (_dot(att_ref[r, h * CHUNK:(h + 1) * CHUNK], v_ref[r, c])
                     + _dot_nt(qe_ref[r, c], (st * e_mid).astype(BF16)))
                so_ref[s, h] = st * d_last + _dot(vt_ref[s, c, :], ke_ref[r, c]) * e_last
                on = o * lax.rsqrt(jnp.mean(o * o, axis=-1, keepdims=True) + EPS) * g_onorm_ref[:, c]
                lhs2_ref[r, c] = (on * sz_ref[r, c]).astype(BF16)
            return piece

        def branch_b_out(c0):
            def piece():
                c = slice(c0, c0 + PW)
                yb = _dot(lhs2_ref[...], _w(w_b_ref, c0, c0 + PW))
                mg_ref[:, c] = (ma_ref[:, c] + sgb_ref[:, c] * yb).astype(BF16)
            return piece

        def out_proj(c0):
            def piece():
                out = _dot(mg_ref[...], _w(w_o_ref, c0, c0 + PW))
                y_ref[:, :, c0:c0 + PW] = out.reshape(NSEQ, CHUNK, PW)
            return piece

        def post_norm(s):
            def piece():
                x1 = xs_ref[rows(s), :] + _rms(y_ref[s], g_post_ref[...])
                y_ref[s] = x1
                lhs2_ref[rows(s), :] = _rms(x1, g_ple_ref[...]).astype(BF16)
            return piece

        def ple(c0):
            def piece():
                gate = _sigmoid(_dot(lhs2_ref[...], _w(w_pg_ref, c0, c0 + PW)))
                pp = _dot(p_ref[...].reshape(TM, D_PLE).astype(BF16), _w(w_pp_ref, c0, c0 + PW))
                y_ref[:, :, c0:c0 + PW] = y_ref[:, :, c0:c0 + PW] + (pp * gate).reshape(NSEQ, CHUNK, PW)
            return piece

        def pre_norm(s):
            def piece():
                hb_ref[rows(s), :] = _rms(x_ref[s], g_pre_ref[...]).astype(BF16)
            return piece

        def in_proj(dst_ref, wc0):
            def make(c0):
                def piece():
                    dst_ref[:, c0:c0 + PW] = _dot(hb_ref[...], _w(w_in_ref, wc0 + c0, wc0 + c0 + PW))
                return piece
            return [make(c0) for c0 in range(0, 2 * GW, PW)]

        def conv_part(s):
            def piece():
                r = rows(s)
                u = pj0_ref[r, 0:GW] * pj0_ref[r, GW:2 * GW]
                uext_ref[s, HIST_ROW:HIST_ROW + CHUNK, :] = u
                cw = conv_w_ref[...]
                ball_ref[r, :] = (cw[0:1, :] * uext_ref[s, HIST_ROW - 2:HIST_ROW - 2 + CHUNK, :]
                                  + cw[1:2, :] * uext_ref[s, HIST_ROW - 1:HIST_ROW - 1 + CHUNK, :]
                                  + cw[2:3, :] * u)
                uext_ref[s, HIST_ROW - 2:HIST_ROW, :] = uext_ref[s, HIST_ROW + CHUNK - 2:HIST_ROW + CHUNK, :]
            return piece

        def gated(s):
            def piece():
                r = rows(s)
                ba = pj1_ref[r, 0:GW]
                za = pj1_ref[r, GW:2 * GW]
                lhs1_ref[r, :] = (za * _sigmoid(za) * ba * ball_ref[r, :]).astype(BF16)
            return piece

        def forget_part(s):
            def piece():
                r = rows(s)
                fg = lb + (1.0 - lb) * _sigmoid(pj0_ref[r, GW:2 * GW])
                pj0_ref[r, GW:2 * GW] = 1.0 - fg
                logf = jnp.log(fg)
                hi = logf.astype(BF16)
                hl_ref[r, :] = hi
                hl_ref[slice(TM + s * CHUNK, TM + (s + 1) * CHUNK), :] = (logf - hi.astype(F32)).astype(BF16)
            return piece

        def cumsum(c0):
            def piece():
                row = lax.broadcasted_iota(jnp.int32, (TM, 2 * TM), 0)
                col = lax.broadcasted_iota(jnp.int32, (TM, 2 * TM), 1) % TM
                ltri = jnp.where((row // CHUNK == col // CHUNK) & (col <= row), 1.0, 0.0).astype(BF16)
                ball_ref[:, c0:c0 + PW] = _dot(ltri, hl_ref[:, c0:c0 + PW])
            return piece

        def decay_part(s):
            def piece():
                r = rows(s)
                b = ball_ref[r, :]
                bmid = b[MID:MID + 1, :]
                blast = b[CHUNK - 1:CHUNK, :]
                q = pj0_ref[r, 0:GW]
                qe_ref[r, :] = (q * _sigmoid(q) * QK_SCALE * jnp.exp(b - bmid)).astype(BF16)
                ke_ref[r, :] = (pj0_ref[r, GW:2 * GW] * jnp.exp(bmid - b)).astype(BF16)
                rows_ref[s, 0:1, :] = jnp.exp(bmid)
                rows_ref[s, 1:2, :] = jnp.exp(blast - bmid)
                rows_ref[s, 2:3, :] = jnp.exp(blast)
            return piece

        def value_part(s):
            def piece():
                r = rows(s)
                iv = pj1_ref[r, 0:GW]
                v_ref[r, :] = iv.astype(BF16)
                vt_ref[s] = iv.T.astype(BF16)
                zb = pj1_ref[r, GW:2 * GW]
                sz_ref[r, :] = zb * _sigmoid(zb)
                xs_ref[r, :] = x_ref[s]
            return piece

        def branch_a_out(c0):
            def piece():
                c = slice(c0, c0 + PW)
                ya = _dot(lhs1_ref[...], _w(w_a_ref, c0, c0 + PW))
                ma_ref[:, c] = _sigmoid(pj0_ref[:, c]) * ya
                sgb_ref[:, c] = _sigmoid(pj0_ref[:, GW + c0:GW + c0 + PW])
            return piece

        _weave([scores(s, h) for s in seqs for h in heads], [pre_norm(s) for s in seqs])
        _weave([hgrn(s, h) for s in seqs for h in heads], in_proj(pj0_ref, C_VC) + in_proj(pj1_ref, C_BZ))
        _weave([branch_b_out(c0) for c0 in cols], [conv_part(s) for s in seqs])
        _weave(in_proj(pj0_ref, C_QF), [gated(s) for s in seqs])
        _weave([out_proj(c0) for c0 in cols], [forget_part(s) for s in seqs])
        _weave(in_proj(pj1_ref, C_IZ) + [cumsum(c0) for c0 in cols], [post_norm(s) for s in seqs])
        _weave([ple(c0) for c0 in cols], [decay_part(s) for s in seqs])
        _weave(in_proj(pj0_ref, C_G) + [branch_a_out(c0) for c0 in cols], [value_part(s) for s in seqs])

        @pl.when(t2 == nt - 1)
        def _store_state():
            for s in range(NSEQ):
                for h in range(H_B):
                    so_ref[s, h] = so_ref[s, h].T

        @pl.when((t1 == nt - 1) & (i < n_tiles))
        def _store_conv():
            convo_ref[...] = uext_ref[:, HIST_ROW - 2:HIST_ROW, :]

    return body


def _run_stream(x, p, conv0, s0, weights, zero_state):
    bsz, tlen, _ = x.shape
    assert bsz % NSEQ == 0 and tlen % CHUNK == 0
    nt = tlen // CHUNK
    n_tiles = (bsz // NSEQ) * nt

    def tile1(i):
        j = jnp.minimum(i, n_tiles - 1)
        return j // nt, j % nt

    def tile2(i):
        j = jnp.maximum(i - 1, 0)
        return j // nt, j % nt

    def resident(arr):
        nd = arr.ndim
        return pl.BlockSpec(arr.shape, lambda i: (0,) * nd, pipeline_mode=pl.Buffered(1))

    in_specs = [
        pl.BlockSpec((NSEQ, CHUNK, D_MODEL), lambda i: (*tile1(i), 0)),
        pl.BlockSpec((NSEQ, CHUNK, D_PLE), lambda i: (*tile2(i), 0)),
        pl.BlockSpec((NSEQ, CONV_W - 1, D_A), lambda i: (tile1(i)[0], 0, 0)),
        (resident(s0) if zero_state else
         pl.BlockSpec((NSEQ, H_B, K_B, V_B), lambda i: (tile2(i)[0], 0, 0, 0), pipeline_mode=pl.Buffered(1))),
    ] + [resident(w) for w in weights]
    out_specs = [
        pl.BlockSpec((NSEQ, CHUNK, D_MODEL), lambda i: (*tile2(i), 0)),
        pl.BlockSpec((NSEQ, CONV_W - 1, D_A), lambda i: (tile1(i)[0], 0, 0)),
        pl.BlockSpec((NSEQ, H_B, K_B, V_B), lambda i: (tile2(i)[0], 0, 0, 0)),
    ]
    out_shape = [
        jax.ShapeDtypeStruct((bsz, tlen, D_MODEL), F32),
        jax.ShapeDtypeStruct((bsz, CONV_W - 1, D_A), F32),
        jax.ShapeDtypeStruct((bsz, H_B, K_B, V_B), F32),
    ]
    scratch = [
        pltpu.VMEM((NSEQ, HIST_ROW + CHUNK, D_A), F32),
        pltpu.VMEM((TM, D_MODEL), BF16),
        pltpu.VMEM((TM, 2 * GW), F32),
        pltpu.VMEM((TM, 2 * GW), F32),
        pltpu.VMEM((TM, D_A), BF16),
        pltpu.VMEM((TM, D_MODEL), BF16),
        pltpu.VMEM((TM, D_MODEL), BF16),
        pltpu.VMEM((TM, D_B), F32),
        pltpu.VMEM((2 * TM, D_B), BF16),
        pltpu.VMEM((TM, H_B * CHUNK), BF16),
        pltpu.VMEM((TM, D_MODEL), F32),
        pltpu.VMEM((TM, D_MODEL), F32),
        pltpu.VMEM((TM, D_MODEL), F32),
        pltpu.VMEM((TM, D_B), F32),
        pltpu.VMEM((TM, D_B), BF16),
        pltpu.VMEM((TM, D_B), BF16),
        pltpu.VMEM((TM, D_B), BF16),
        pltpu.VMEM((NSEQ, D_B, CHUNK), BF16),
        pltpu.VMEM((NSEQ, 8, D_B), F32),
    ]
    return pl.pallas_call(
        _make_kernel(n_tiles, nt, zero_state),
        grid=(n_tiles + 1,),
        in_specs=in_specs,
        out_specs=out_specs,
        out_shape=out_shape,
        scratch_shapes=scratch,
        compiler_params=pltpu.CompilerParams(
            dimension_semantics=("arbitrary",),
            vmem_limit_bytes=V7X_VMEM_LIMIT_BYTES,
        ),
    )(x, p, conv0, s0, *weights)


def _pack_kernel(w_ref, o_ref):
    o_ref[...] = pltpu.bitcast(w_ref[...].astype(BF16), jnp.uint32)


def _pack_rows(w, col_block_map=None):
    k, n = w.shape
    bk, bn = 256, min(n, GW)
    cmap = col_block_map or (lambda c: c)
    return pl.pallas_call(
        _pack_kernel,
        grid=(k // bk, n // bn),
        in_specs=[pl.BlockSpec((bk, bn), lambda r, c: (r, cmap(c)))],
        out_specs=pl.BlockSpec((bk // 2, bn), lambda r, c: (r, c)),
        out_shape=jax.ShapeDtypeStruct((k // 2, n), jnp.uint32),
    )(w)


def _w_in_block(c):
    assert [SPLIT_NAMES.index(nm) for nm in KERNEL_ORDER] == [0, 2, 1, 3, 4, 5, 6, 7, 8, 9]
    return jnp.where((c == 1) | (c == 2), 3 - c, c)


def kernel(x_prompt, x_sample, state_conv, state_hgrn, p_prompt, p_sample, w_in, conv_w, lb_raw, g_pre,
           g_onorm, w_a_out, w_b_out, w_o, g_post, g_ple, w_ple_gate, w_ple_proj):
    assert w_in.shape[0] == 1 and w_in.shape[2] == N_IN
    i = 0
    weights = (
        _pack_rows(w_in[i], _w_in_block), conv_w[i], lb_raw[i:i + 2], g_pre[i:i + 1], g_onorm[i:i + 1],
        _pack_rows(w_a_out[i]), _pack_rows(w_b_out[i]), _pack_rows(w_o[i]),
        g_post[i:i + 1], g_ple[i:i + 1], _pack_rows(w_ple_gate[i]), _pack_rows(w_ple_proj[i]),
    )
    bp = x_prompt.shape[0]
    zero_conv = jnp.zeros((bp, CONV_W - 1, D_A), F32)
    dummy_s = jnp.zeros((1, 1, 8, V_B), F32)
    yp, cp, sp = _run_stream(x_prompt, p_prompt[i], zero_conv, dummy_s, weights, zero_state=True)
    ys, cs, ss = _run_stream(x_sample, p_sample[i], state_conv[i], state_hgrn[i], weights, zero_state=False)
    return (yp, ys, cp[None], sp[None], cs[None], ss[None])
```

```python
import jax
import jax.numpy as jnp
from jax import lax
from jax.experimental import pallas as pl
from jax.experimental.pallas import tpu as pltpu

D_MODEL = 1024
D_PLE = 256
D_A = 1024
CONV_W = 3
H_B = 8
K_B = 128
V_B = 128
D_B = H_B * V_B
CHUNK = 64
EPS = 1e-6
QK_SCALE = K_B ** -0.5

NSEQ = 4
TM = NSEQ * CHUNK
HIST_ROW = 8
MID = (CHUNK - 1) // 2
PW = 512

SPLIT_NAMES = ("vA", "bA", "cA", "zA", "q", "f", "iv", "zB", "gA", "gB")
KERNEL_ORDER = ("vA", "cA", "bA", "zA", "q", "f", "iv", "zB", "gA", "gB")
GW = 1024
N_IN = GW * len(SPLIT_NAMES)
C_VC, C_BZ, C_QF, C_IZ, C_G = 0, 2 * GW, 4 * GW, 6 * GW, 8 * GW

V7X_VMEM_LIMIT_BYTES = 60 * 1024 * 1024

F32 = jnp.float32
BF16 = jnp.bfloat16


def _dot(a, b):
    return jnp.dot(a, b, preferred_element_type=F32)


def _dot_nt(a, b):
    return lax.dot_general(a, b, (((1,), (1,)), ((), ())), preferred_element_type=F32)


def _w(ref, c0, c1):
    return pltpu.bitcast(ref[:, c0:c1], BF16)


def _sigmoid(x):
    return 1.0 / (1.0 + jnp.exp(-x))


def _rms(x, g):
    return x * lax.rsqrt(jnp.mean(x * x, axis=-1, keepdims=True) + EPS) * g


def _weave(*piece_lists):
    pos = [0] * len(piece_lists)
    for _ in range(sum(len(pl_) for pl_ in piece_lists)):
        k = min((j for j in range(len(piece_lists)) if pos[j] < len(piece_lists[j])),
                key=lambda j: (pos[j] + 0.5) / len(piece_lists[j]))
        piece_lists[k][pos[k]]()
        pos[k] += 1


def _make_kernel(n_tiles, nt, zero_state):
    def body(x_ref, p_ref, conv0_ref, s0_ref, w_in_ref, conv_w_ref, lb_raw_ref, g_pre_ref,
             g_onorm_ref, w_a_ref, w_b_ref, w_o_ref, g_post_ref, g_ple_ref, w_pg_ref, w_pp_ref,
             y_ref, convo_ref, so_ref,
             uext_ref, hb_ref, pj0_ref, pj1_ref, lhs1_ref, lhs2_ref, mg_ref, ball_ref, hl_ref, att_ref,
             xs_ref, ma_ref, sgb_ref, sz_ref, qe_ref, ke_ref, v_ref, vt_ref, rows_ref):
        i = pl.program_id(0)
        j1 = jnp.minimum(i, n_tiles - 1)
        j2 = jnp.maximum(i - 1, 0)
        t1 = j1 % nt
        t2 = j2 % nt

        @pl.when(i == 0)
        def _zero_staging():
            for r in (xs_ref, ma_ref, sgb_ref, sz_ref, qe_ref, ke_ref, v_ref, vt_ref, rows_ref):
                r[...] = jnp.zeros(r.shape, r.dtype)

        @pl.when(t2 == 0)
        def _load_state():
            for s in range(NSEQ):
                for h in range(H_B):
                    if zero_state:
                        so_ref[s, h] = jnp.zeros((V_B, K_B), F32)
                    else:
                        so_ref[s, h] = s0_ref[s, h].T

        @pl.when(t1 == 0)
        def _load_conv():
            uext_ref[:, HIST_ROW - 2:HIST_ROW, :] = conv0_ref[...]

        ti = lax.broadcasted_iota(jnp.int32, (CHUNK, CHUNK), 0)
        si = lax.broadcasted_iota(jnp.int32, (CHUNK, CHUNK), 1)
        causal = si <= ti
        lbr = lb_raw_ref[...]
        lbe = jnp.exp(lbr - jnp.max(lbr, axis=0, keepdims=True))
        lb = lbe[0:1, :] / jnp.sum(lbe, axis=0, keepdims=True)
        seqs = range(NSEQ)
        heads = range(H_B)
        cols = range(0, D_MODEL, PW)

        def rows(s):
            return slice(s * CHUNK, (s + 1) * CHUNK)

        def scores(s, h):
            def piece():
                r, c = rows(s), slice(h * K_B, (h + 1) * K_B)
                att = jnp.where(causal, _dot_nt(qe_ref[r, c], ke_ref[r, c]), 0.0)
                att_ref[r, h * CHUNK:(h + 1) * CHUNK] = att.astype(BF16)
            return piece

        def hgrn(s, h):
            def piece():
                r, c = rows(s), slice(h * K_B, (h + 1) * K_B)
                st = so_ref[s, h]
                e_mid = rows_ref[s, 0:1, c]
                e_last = rows_ref[s, 1:2, c]
                d_last = rows_ref[s, 2:3, c]
                o = (_dot(att_ref[r, h * CHUNK:(h + 1) * CHUNK], v_ref[r, c])
                     + _dot_nt(qe_ref[r, c], (st * e_mid).astype(BF16)))
                so_ref[s, h] = st * d_last + _dot(vt_ref[s, c, :], ke_ref[r, c]) * e_last
                on = o * lax.rsqrt(jnp.mean(o * o, axis=-1, keepdims=True) + EPS) * g_onorm_ref[:, c]
                lhs2_ref[r, c] = (on * sz_ref[r, c]).astype(BF16)
            return piece

        def branch_b_out(c0):
            def piece():
                c = slice(c0, c0 + PW)
                yb = _dot(lhs2_ref[...], _w(w_b_ref, c0, c0 + PW))
                mg_ref[:, c] = (ma_ref[:, c] + sgb_ref[:, c] * yb).astype(BF16)
            return piece

        def out_proj(c0):
            def piece():
                out = _dot(mg_ref[...], _w(w_o_ref, c0, c0 + PW))
                y_ref[:, :, c0:c0 + PW] = out.reshape(NSEQ, CHUNK, PW)
            return piece

        def post_norm(s):
            def piece():
                x1 = xs_ref[rows(s), :] + _rms(y_ref[s], g_post_ref[...])
                y_ref[s] = x1
                lhs2_ref[rows(s), :] = _rms(x1, g_ple_ref[...]).astype(BF16)
            return piece

        def ple(c0):
            def piece():
                gate = _sigmoid(_dot(lhs2_ref[...], _w(w_pg_ref, c0, c0 + PW)))
                pp = _dot(p_ref[...].reshape(TM, D_PLE).astype(BF16), _w(w_pp_ref, c0, c0 + PW))
                y_ref[:, :, c0:c0 + PW] = y_ref[:, :, c0:c0 + PW] + (pp * gate).reshape(NSEQ, CHUNK, PW)
            return piece

        def pre_norm(s):
            def piece():
                hb_ref[rows(s), :] = _rms(x_ref[s], g_pre_ref[...]).astype(BF16)
            return piece

        def in_proj(dst_ref, wc0):
            def make(c0):
                def piece():
                    dst_ref[:, c0:c0 + PW] = _dot(hb_ref[...], _w(w_in_ref, wc0 + c0, wc0 + c0 + PW))
                return piece
            return [make(c0) for c0 in range(0, 2 * GW, PW)]

        def conv_part(s):
            def piece():
                r = rows(s)
                u = pj0_ref[r, 0:GW] * pj0_ref[r, GW:2 * GW]
                uext_ref[s, HIST_ROW:HIST_ROW + CHUNK, :] = u
                cw = conv_w_ref[...]
                ball_ref[r, :] = (cw[0:1, :] * uext_ref[s, HIST_ROW - 2:HIST_ROW - 2 + CHUNK, :]
                                  + cw[1:2, :] * uext_ref[s, HIST_ROW - 1:HIST_ROW - 1 + CHUNK, :]
                                  + cw[2:3, :] * u)
                uext_ref[s, HIST_ROW - 2:HIST_ROW, :] = uext_ref[s, HIST_ROW + CHUNK - 2:HIST_ROW + CHUNK, :]
            return piece

        def gated(s):
            def piece():
                r = rows(s)
                ba = pj1_ref[r, 0:GW]
                za = pj1_ref[r, GW:2 * GW]
                lhs1_ref[r, :] = (za * _sigmoid(za) * ba * ball_ref[r, :]).astype(BF16)
            return piece

        def forget_part(s):
            def piece():
                r = rows(s)
                fg = lb + (1.0 - lb) * _sigmoid(pj0_ref[r, GW:2 * GW])
                pj0_ref[r, GW:2 * GW] = 1.0 - fg
                logf = jnp.log(fg)
                hi = logf.astype(BF16)
                hl_ref[r, :] = hi
                hl_ref[slice(TM + s * CHUNK, TM + (s + 1) * CHUNK), :] = (logf - hi.astype(F32)).astype(BF16)
            return piece

        def cumsum(c0):
            def piece():
                row = lax.broadcasted_iota(jnp.int32, (TM, 2 * TM), 0)
                col = lax.broadcasted_iota(jnp.int32, (TM, 2 * TM), 1) % TM
                ltri = jnp.where((row // CHUNK == col // CHUNK) & (col <= row), 1.0, 0.0).astype(BF16)
                ball_ref[:, c0:c0 + PW] = _dot(ltri, hl_ref[:, c0:c0 + PW])
            return piece

        def decay_part(s):
            def piece():
                r = rows(s)
                b = ball_ref[r, :]
                bmid = b[MID:MID + 1, :]
                blast = b[CHUNK - 1:CHUNK, :]
                q = pj0_ref[r, 0:GW]
                qe_ref[r, :] = (q * _sigmoid(q) * QK_SCALE * jnp.exp(b - bmid)).astype(BF16)
                ke_ref[r, :] = (pj0_ref[r, GW:2 * GW] * jnp.exp(bmid - b)).astype(BF16)
                rows_ref[s, 0:1, :] = jnp.exp(bmid)
                rows_ref[s, 1:2, :] = jnp.exp(blast - bmid)
                rows_ref[s, 2:3, :] = jnp.exp(blast)
            return piece

        def value_part(s):
            def piece():
                r = rows(s)
                iv = pj1_ref[r, 0:GW]
                v_ref[r, :] = iv.astype(BF16)
                vt_ref[s] = iv.T.astype(BF16)
                zb = pj1_ref[r, GW:2 * GW]
                sz_ref[r, :] = zb * _sigmoid(zb)
                xs_ref[r, :] = x_ref[s]
            return piece

        def branch_a_out(c0):
            def piece():
                c = slice(c0, c0 + PW)
                ya = _dot(lhs1_ref[...], _w(w_a_ref, c0, c0 + PW))
                ma_ref[:, c] = _sigmoid(pj0_ref[:, c]) * ya
                sgb_ref[:, c] = _sigmoid(pj0_ref[:, GW + c0:GW + c0 + PW])
            return piece

        _weave([scores(s, h) for s in seqs for h in heads], [pre_norm(s) for s in seqs])
        _weave([hgrn(s, h) for s in seqs for h in heads], in_proj(pj0_ref, C_VC) + in_proj(pj1_ref, C_BZ))
        _weave([branch_b_out(c0) for c0 in cols], [conv_part(s) for s in seqs])
        _weave(in_proj(pj0_ref, C_QF), [gated(s) for s in seqs])
        _weave([out_proj(c0) for c0 in cols], [forget_part(s) for s in seqs])
        _weave(in_proj(pj1_ref, C_IZ) + [cumsum(c0) for c0 in cols], [post_norm(s) for s in seqs])
        _weave([ple(c0) for c0 in cols], [decay_part(s) for s in seqs])
        _weave(in_proj(pj0_ref, C_G) + [branch_a_out(c0) for c0 in cols], [value_part(s) for s in seqs])

        @pl.when(t2 == nt - 1)
        def _store_state():
            for s in range(NSEQ):
                for h in range(H_B):
                    so_ref[s, h] = so_ref[s, h].T

        @pl.when((t1 == nt - 1) & (i < n_tiles))
        def _store_conv():
            convo_ref[...] = uext_ref[:, HIST_ROW - 2:HIST_ROW, :]

    return body


def _run_stream(x, p, conv0, s0, weights, zero_state):
    bsz, tlen, _ = x.shape
    assert bsz % NSEQ == 0 and tlen % CHUNK == 0
    nt = tlen // CHUNK
    n_tiles = (bsz // NSEQ) * nt

    def tile1(i):
        j = jnp.minimum(i, n_tiles - 1)
        return j // nt, j % nt

    def tile2(i):
        j = jnp.maximum(i - 1, 0)
        return j // nt, j % nt

    def resident(arr):
        nd = arr.ndim
        return pl.BlockSpec(arr.shape, lambda i: (0,) * nd, pipeline_mode=pl.Buffered(1))

    in_specs = [
        pl.BlockSpec((NSEQ, CHUNK, D_MODEL), lambda i: (*tile1(i), 0)),
        pl.BlockSpec((NSEQ, CHUNK, D_PLE), lambda i: (*tile2(i), 0)),
        pl.BlockSpec((NSEQ, CONV_W - 1, D_A), lambda i: (tile1(i)[0], 0, 0)),
        (resident(s0) if zero_state else
         pl.BlockSpec((NSEQ, H_B, K_B, V_B), lambda i: (tile2(i)[0], 0, 0, 0), pipeline_mode=pl.Buffered(1))),
    ] + [resident(w) for w in weights]
    out_specs = [
        pl.BlockSpec((NSEQ, CHUNK, D_MODEL), lambda i: (*tile2(i), 0)),
        pl.BlockSpec((NSEQ, CONV_W - 1, D_A), lambda i: (tile1(i)[0], 0, 0)),
        pl.BlockSpec((NSEQ, H_B, K_B, V_B), lambda i: (tile2(i)[0], 0, 0, 0)),
    ]
    out_shape = [
        jax.ShapeDtypeStruct((bsz, tlen, D_MODEL), F32),
        jax.ShapeDtypeStruct((bsz, CONV_W - 1, D_A), F32),
        jax.ShapeDtypeStruct((bsz, H_B, K_B, V_B), F32),
    ]
    scratch = [
        pltpu.VMEM((NSEQ, HIST_ROW + CHUNK, D_A), F32),
        pltpu.VMEM((TM, D_MODEL), BF16),
        pltpu.VMEM((TM, 2 * GW), F32),
        pltpu.VMEM((TM, 2 * GW), F32),
        pltpu.VMEM((TM, D_A), BF16),
        pltpu.VMEM((TM, D_MODEL), BF16),
        pltpu.VMEM((TM, D_MODEL), BF16),
        pltpu.VMEM((TM, D_B), F32),
        pltpu.VMEM((2 * TM, D_B), BF16),
        pltpu.VMEM((TM, H_B * CHUNK), BF16),
        pltpu.VMEM((TM, D_MODEL), F32),
        pltpu.VMEM((TM, D_MODEL), F32),
        pltpu.VMEM((TM, D_MODEL), F32),
        pltpu.VMEM((TM, D_B), F32),
        pltpu.VMEM((TM, D_B), BF16),
        pltpu.VMEM((TM, D_B), BF16),
        pltpu.VMEM((TM, D_B), BF16),
        pltpu.VMEM((NSEQ, D_B, CHUNK), BF16),
        pltpu.VMEM((NSEQ, 8, D_B), F32),
    ]
    return pl.pallas_call(
        _make_kernel(n_tiles, nt, zero_state),
        grid=(n_tiles + 1,),
        in_specs=in_specs,
        out_specs=out_specs,
        out_shape=out_shape,
        scratch_shapes=scratch,
        compiler_params=pltpu.CompilerParams(
            dimension_semantics=("arbitrary",),
            vmem_limit_bytes=V7X_VMEM_LIMIT_BYTES,
        ),
    )(x, p, conv0, s0, *weights)


PACK_ROWS = 128


def _pack_kernel(w_in_ref, *refs):
    n = len(refs) // 2
    def pack(x):
        return pltpu.bitcast(x.astype(BF16), jnp.uint32)
    o_in_ref = refs[n]
    for j, name in enumerate(KERNEL_ORDER):
        src = SPLIT_NAMES.index(name)
        o_in_ref[:, j * GW:(j + 1) * GW] = pack(w_in_ref[:, src * GW:(src + 1) * GW])
    for w_ref, o_ref in zip(refs[:n], refs[n + 1:]):
        o_ref[...] = pack(w_ref[...])


def _pack_weights(w_in, others):
    ws = (w_in,) + tuple(others)
    steps = w_in.shape[0] // PACK_ROWS

    def spec(w, rows):
        last = w.shape[0] // PACK_ROWS - 1
        return pl.BlockSpec((rows, w.shape[1]), lambda r: (jnp.minimum(r, last), 0))

    return pl.pallas_call(
        _pack_kernel,
        grid=(steps,),
        in_specs=[spec(w, PACK_ROWS) for w in ws],
        out_specs=[spec(w, PACK_ROWS // 2) for w in ws],
        out_shape=[jax.ShapeDtypeStruct((w.shape[0] // 2, w.shape[1]), jnp.uint32) for w in ws],
        compiler_params=pltpu.CompilerParams(dimension_semantics=("arbitrary",),
                                             vmem_limit_bytes=V7X_VMEM_LIMIT_BYTES),
    )(*ws)


def kernel(x_prompt, x_sample, state_conv, state_hgrn, p_prompt, p_sample, w_in, conv_w, lb_raw, g_pre,
           g_onorm, w_a_out, w_b_out, w_o, g_post, g_ple, w_ple_gate, w_ple_proj):
    assert w_in.shape[0] == 1 and w_in.shape[2] == N_IN
    i = 0
    pw_in, pw_a, pw_b, pw_o, pw_pg, pw_pp = _pack_weights(
        w_in[i], (w_a_out[i], w_b_out[i], w_o[i], w_ple_gate[i], w_ple_proj[i]))
    weights = (pw_in, conv_w[i], lb_raw[i:i + 2], g_pre[i:i + 1], g_onorm[i:i + 1],
               pw_a, pw_b, pw_o, g_post[i:i + 1], g_ple[i:i + 1], pw_pg, pw_pp)
    bp = x_prompt.shape[0]
    zero_conv = jnp.zeros((bp, CONV_W - 1, D_A), F32)
    dummy_s = jnp.zeros((1, 1, 8, V_B), F32)
    yp, cp, sp = _run_stream(x_prompt, p_prompt[i], zero_conv, dummy_s, weights, zero_state=True)
    ys, cs, ss = _run_stream(x_sample, p_sample[i], state_conv[i], state_hgrn[i], weights, zero_state=False)
    return (yp, ys, cp[None], sp[None], cs[None], ss[None])
```
